```python
import math
import jax, jax.numpy as jnp
from jax import lax
import numpy as np

D_MODEL = 1024
BATCH = 2
SEQ = 16384
DEPTH = 1
DEC_BATCH = 8
DEC_SEQ = 16
PAST_LEN = 1024

CHUNK = 64
MIX_WIDTH = D_MODEL
RWKV_WIDTH = MIX_WIDTH // 2
HEAD_DIM = 64
RWKV_HEADS = RWKV_WIDTH // HEAD_DIM
W_LORA = 64
A_LORA = 64
G_LORA = 128
RWKV_PROJ = 3 * RWKV_WIDTH + W_LORA + A_LORA + G_LORA
S5_WIDTH = MIX_WIDTH - RWKV_WIDTH
S5_CH = 16
S5_GROUPS = S5_WIDTH // S5_CH
S5_STATE = 64
IN_PROJ = RWKV_PROJ + S5_WIDTH
PEER_HEADS = 8
N_KEYS = 128
N_EXPERTS = N_KEYS * N_KEYS
PEER_TOPK = 16
PEER_QDIM = 256
PEER_HALF = PEER_QDIM // 2
PEER_BLOCK = 512
NORM_EPS = 1e-6
GN_EPS = HEAD_DIM * 1e-5

kernel_name = "hybrid_rwkv7_s5_peer_stream_step"


def rmsnorm(x, g):
    xf = x.astype(jnp.float32)
    y = xf * lax.rsqrt(jnp.mean(xf * xf, axis=-1, keepdims=True) + NORM_EPS) * g.astype(jnp.float32)
    return y.astype(x.dtype)


def wkv_scan(r, w, k, v, a, b, s0):
    bt, t_len, nh, nd = r.shape
    blk = CHUNK if t_len % CHUNK == 0 else t_len

    def step(s, inp):
        r_t, w_t, k_t, v_t, a_t, b_t = inp
        sa = jnp.einsum('bhvk,bhk->bhv', s, a_t)
        s = s * w_t[:, :, None, :] + sa[..., None] * b_t[:, :, None, :] + v_t[..., None] * k_t[:, :, None, :]
        return s, jnp.einsum('bhvk,bhk->bhv', s, r_t)

    def block(s, inp):
        return lax.scan(step, s, inp)

    xs = tuple(z.transpose(1, 0, 2, 3).reshape(t_len // blk, blk, bt, nh, nd) for z in (r, w, k, v, a, b))
    s1, o = lax.scan(block, s0, xs)
    o = o.reshape(t_len, bt, nh, nd).transpose(1, 0, 2, 3)
    return o, s1


def rwkv_time_mix(p, shift0, s0, prm):
    bt, t_len, _ = p.shape
    f32 = jnp.float32
    p_prev = jnp.concatenate([shift0.astype(p.dtype), p[:, :-1]], axis=1)
    m = (p + (p_prev - p) * prm['rwkv_mu']).astype(f32)
    r, k, v, wl, al, gl = jnp.split(m, [RWKV_WIDTH, 2 * RWKV_WIDTH, 3 * RWKV_WIDTH,
                                        3 * RWKV_WIDTH + W_LORA, 3 * RWKV_WIDTH + W_LORA + A_LORA], axis=-1)
    w = -jax.nn.softplus(-(prm['rwkv_w0'] + jnp.tanh(wl) @ prm['rwkv_w2'])) - 0.5
    decay = jnp.exp(-jnp.exp(w))
    a = jax.nn.sigmoid(prm['rwkv_a0'] + al @ prm['rwkv_a2'])
    g = jax.nn.sigmoid(gl) @ prm['rwkv_g2']
    hs = lambda z: z.reshape(bt, t_len, RWKV_HEADS, HEAD_DIM)
    kk = hs(k * prm['rwkv_k_k'])
    kk = kk / jnp.maximum(jnp.sqrt(jnp.sum(kk * kk, axis=-1, keepdims=True)), 1e-12)
    k = k * (1.0 + (a - 1.0) * prm['rwkv_k_a'])
    rh, kh, vh, ah = hs(r), hs(k), hs(v), hs(a)
    o, s1 = wkv_scan(rh, hs(decay), kh, vh, -kk, kk * ah, s0.astype(f32))
    mu = jnp.mean(o, axis=-1, keepdims=True)
    var = jnp.mean(jnp.square(o - mu), axis=-1, keepdims=True)
    o = ((o - mu) * lax.rsqrt(var + GN_EPS)).reshape(bt, t_len, RWKV_WIDTH) * prm['rwkv_gn_w'] + prm['rwkv_gn_b']
    bonus = jnp.sum(rh * kh * prm['rwkv_r_k'].astype(f32), axis=-1, keepdims=True) * vh
    o = (o + bonus.reshape(bt, t_len, RWKV_WIDTH)) * g
    return o, s1, p[:, -1:]


def _lin_rec_combine(e1, e2):
    a1, b1 = e1
    a2, b2 = e2
    return a1 * a2, a2 * b1 + b2


def s5_mix(u, re0, im0, prm):
    bt, t_len, _ = u.shape
    f32 = jnp.float32
    uf = u.astype(f32).reshape(bt, t_len, S5_GROUPS, S5_CH)
    lam = lax.complex(prm['s5_a_re'].astype(f32), prm['s5_a_im'].astype(f32))
    dt = jnp.exp(prm['s5_log_dt'].astype(f32))[:, None]
    lam_bar = jnp.exp(lam * dt)
    b_bar = ((lam_bar - 1.0) / lam)[..., None] * lax.complex(prm['s5_b_re'].astype(f32), prm['s5_b_im'].astype(f32))
    c = lax.complex(prm['s5_c_re'].astype(f32), prm['s5_c_im'].astype(f32))
    bu = jnp.einsum('gpc,btgc->btgp', b_bar, uf.astype(jnp.complex64))
    x0 = lax.complex(re0.astype(f32), im0.astype(f32))
    bu = bu.at[:, 0].add(lam_bar * x0)
    a_el = jnp.broadcast_to(lam_bar, bu.shape)
    _, xs = lax.associative_scan(_lin_rec_combine, (a_el, bu), axis=1)
    y = jnp.real(jnp.einsum('gcp,btgp->btgc', c, xs)) + prm['s5_d'].astype(f32).reshape(S5_GROUPS, S5_CH) * uf
    y = jax.nn.gelu(y.reshape(bt, t_len, S5_WIDTH))
    y = y * jax.nn.sigmoid(y @ prm['s5_w_glu'] + prm['s5_b_glu'])
    last = xs[:, -1]
    return y, jnp.real(last), jnp.imag(last)


def peer_ffn(h, prm):
    bt, t_len, d = h.shape
    n = bt * t_len
    blk = min(PEER_BLOCK, n)
    n_pad = (-n) % blk
    hb = jnp.pad(h.reshape(n, d), ((0, n_pad), (0, 0))).reshape(-1, blk, d)
    f32 = jnp.float32
    w_q, k1, k2 = prm['peer_w_q'], prm['peer_keys1'].astype(f32), prm['peer_keys2'].astype(f32)
    u_tab, v_tab = prm['peer_u'], prm['peer_v']

    def one_block(xb):
        q = (xb @ w_q).astype(f32).reshape(blk, PEER_HEADS, 2, PEER_HALF)
        s1 = jnp.einsum('thd,hnd->thn', q[:, :, 0], k1)
        s2 = jnp.einsum('thd,hnd->thn', q[:, :, 1], k2)
        v1, i1 = lax.top_k(s1, PEER_TOPK)
        v2, i2 = lax.top_k(s2, PEER_TOPK)
        cand = (v1[..., :, None] + v2[..., None, :]).reshape(blk, PEER_HEADS, PEER_TOPK * PEER_TOPK)
        cidx = (i1[..., :, None] * N_KEYS + i2[..., None, :]).reshape(blk, PEER_HEADS, PEER_TOPK * PEER_TOPK)
        sc, pos = lax.top_k(cand, PEER_TOPK)
        idx = jnp.take_along_axis(cidx, pos, axis=-1)
        gate = jax.nn.softmax(sc, axis=-1)
        act = jax.nn.gelu(jnp.einsum('thkd,td->thk', jnp.take(u_tab, idx, axis=0), xb).astype(f32))
        coeff = (gate * act).astype(xb.dtype)
        return jnp.einsum('thk,thkd->td', coeff, jnp.take(v_tab, idx, axis=0))

    out = lax.map(one_block, hb).reshape(-1, d)[:n]
    return out.reshape(bt, t_len, d).astype(h.dtype)


def hybrid_layer(x, wkv0, shift0, s5re0, s5im0, prm):
    h = rmsnorm(x, prm['ln1_g'])
    proj = h @ prm['w_in']
    o_rw, wkv1, shift1 = rwkv_time_mix(proj[..., :RWKV_PROJ], shift0, wkv0, prm)
    o_s5, re1, im1 = s5_mix(proj[..., RWKV_PROJ:], s5re0, s5im0, prm)
    mixed = jnp.concatenate([o_rw, o_s5], axis=-1).astype(x.dtype)
    x = x + mixed @ prm['w_out']
    x = x + peer_ffn(rmsnorm(x, prm['ln2_g']), prm)
    return x, wkv1, shift1, re1, im1


def setup_inputs(seed: int = 0) -> dict:
    key = jax.random.key(seed)
    ks = iter(jax.random.split(key, 40))
    f32 = jnp.float32
    nrm = lambda shape, scale: jax.random.normal(next(ks), shape, f32) * scale
    L = DEPTH
    w0 = jnp.broadcast_to(jnp.linspace(-5.5, -0.5, RWKV_WIDTH, dtype=f32), (L, RWKV_WIDTH)) + nrm((L, RWKV_WIDTH), 0.05)
    a_im = jnp.broadcast_to(jnp.pi * jnp.arange(S5_STATE, dtype=f32), (L, S5_GROUPS, S5_STATE)) + nrm((L, S5_GROUPS, S5_STATE), 0.01)
    log_dt = jax.random.uniform(next(ks), (L, S5_GROUPS), f32, math.log(1e-3), math.log(1e-1))
    return {
        'x_prompt': nrm((BATCH, SEQ, D_MODEL), 1.0),
        'x_sample': nrm((DEC_BATCH, DEC_SEQ, D_MODEL), 1.0),
        'state_wkv': nrm((L, DEC_BATCH, RWKV_HEADS, HEAD_DIM, HEAD_DIM), 0.5),
        'state_shift': nrm((L, DEC_BATCH, 1, RWKV_PROJ), 1.0),
        'state_s5_re': nrm((L, DEC_BATCH, S5_GROUPS, S5_STATE), 0.1),
        'state_s5_im': nrm((L, DEC_BATCH, S5_GROUPS, S5_STATE), 0.1),
        'ln1_g': 1.0 + nrm((L, D_MODEL), 0.02),
        'w_in': nrm((L, D_MODEL, IN_PROJ), D_MODEL ** -0.5),
        'rwkv_mu': jax.random.uniform(next(ks), (L, RWKV_PROJ), f32),
        'rwkv_w0': w0,
        'rwkv_w2': nrm((L, W_LORA, RWKV_WIDTH), 0.1 * W_LORA ** -0.5),
        'rwkv_a0': nrm((L, RWKV_WIDTH), 0.1),
        'rwkv_a2': nrm((L, A_LORA, RWKV_WIDTH), 0.3 * A_LORA ** -0.5),
        'rwkv_g2': nrm((L, G_LORA, RWKV_WIDTH), G_LORA ** -0.5),
        'rwkv_k_k': 0.85 + nrm((L, RWKV_WIDTH), 0.02),
        'rwkv_k_a': 1.0 + nrm((L, RWKV_WIDTH), 0.02),
        'rwkv_r_k': nrm((L, RWKV_HEADS, HEAD_DIM), 0.1),
        'rwkv_gn_w': 1.0 + nrm((L, RWKV_WIDTH), 0.02),
        'rwkv_gn_b': nrm((L, RWKV_WIDTH), 0.01),
        's5_a_re': -0.5 + nrm((L, S5_GROUPS, S5_STATE), 0.01),
        's5_a_im': a_im,
        's5_log_dt': log_dt,
        's5_b_re': nrm((L, S5_GROUPS, S5_STATE, S5_CH), (2 * S5_CH) ** -0.5),
        's5_b_im': nrm((L, S5_GROUPS, S5_STATE, S5_CH), (2 * S5_CH) ** -0.5),
        's5_c_re': nrm((L, S5_GROUPS, S5_CH, S5_STATE), (2 * S5_STATE) ** -0.5),
        's5_c_im': nrm((L, S5_GROUPS, S5_CH, S5_STATE), (2 * S5_STATE) ** -0.5),
        's5_d': nrm((L, S5_WIDTH), 1.0),
        's5_w_glu': nrm((L, S5_WIDTH, S5_WIDTH), S5_WIDTH ** -0.5),
        's5_b_glu': nrm((L, S5_WIDTH), 0.01),
        'w_out': nrm((L, MIX_WIDTH, D_MODEL), MIX_WIDTH ** -0.5),
        'ln2_g': 1.0 + nrm((L, D_MODEL), 0.02),
        'peer_w_q': nrm((L, D_MODEL, PEER_HEADS * PEER_QDIM), D_MODEL ** -0.5),
        'peer_keys1': nrm((L, PEER_HEADS, N_KEYS, PEER_HALF), PEER_HALF ** -0.5),
        'peer_keys2': nrm((L, PEER_HEADS, N_KEYS, PEER_HALF), PEER_HALF ** -0.5),
        'peer_u': nrm((L, N_EXPERTS, D_MODEL), D_MODEL ** -0.5),
        'peer_v': nrm((L, N_EXPERTS, D_MODEL), PEER_HEADS ** -0.5),
        'lnf_g': 1.0 + nrm((D_MODEL,), 0.02),
    }


def reference(x_prompt, x_sample, state_wkv, state_shift, state_s5_re, state_s5_im,
              ln1_g, w_in, rwkv_mu, rwkv_w0, rwkv_w2, rwkv_a0, rwkv_a2, rwkv_g2,
              rwkv_k_k, rwkv_k_a, rwkv_r_k, rwkv_gn_w, rwkv_gn_b,
              s5_a_re, s5_a_im, s5_log_dt, s5_b_re, s5_b_im, s5_c_re, s5_c_im, s5_d,
              s5_w_glu, s5_b_glu, w_out, ln2_g, peer_w_q, peer_keys1, peer_keys2,
              peer_u, peer_v, lnf_g):
    f32 = jnp.float32
    bp = x_prompt.shape[0]
    yp, ys = x_prompt, x_sample
    wkv_p, sh_p, re_p, im_p = [], [], [], []
    wkv_s, sh_s, re_s, im_s = [], [], [], []
    for l in range(DEPTH):
        prm = {
            'ln1_g': ln1_g[l], 'w_in': w_in[l], 'rwkv_mu': rwkv_mu[l], 'rwkv_w0': rwkv_w0[l],
            'rwkv_w2': rwkv_w2[l], 'rwkv_a0': rwkv_a0[l], 'rwkv_a2': rwkv_a2[l], 'rwkv_g2': rwkv_g2[l],
            'rwkv_k_k': rwkv_k_k[l], 'rwkv_k_a': rwkv_k_a[l], 'rwkv_r_k': rwkv_r_k[l],
            'rwkv_gn_w': rwkv_gn_w[l], 'rwkv_gn_b': rwkv_gn_b[l],
            's5_a_re': s5_a_re[l], 's5_a_im': s5_a_im[l], 's5_log_dt': s5_log_dt[l],
            's5_b_re': s5_b_re[l], 's5_b_im': s5_b_im[l], 's5_c_re': s5_c_re[l], 's5_c_im': s5_c_im[l],
            's5_d': s5_d[l], 's5_w_glu': s5_w_glu[l], 's5_b_glu': s5_b_glu[l], 'w_out': w_out[l],
            'ln2_g': ln2_g[l], 'peer_w_q': peer_w_q[l], 'peer_keys1': peer_keys1[l],
            'peer_keys2': peer_keys2[l], 'peer_u': peer_u[l], 'peer_v': peer_v[l],
        }
        yp, a1, a2, a3, a4 = hybrid_layer(
            yp,
            jnp.zeros((bp, RWKV_HEADS, HEAD_DIM, HEAD_DIM), f32),
            jnp.zeros((bp, 1, RWKV_PROJ), yp.dtype),
            jnp.zeros((bp, S5_GROUPS, S5_STATE), f32),
            jnp.zeros((bp, S5_GROUPS, S5_STATE), f32),
            prm)
        wkv_p.append(a1); sh_p.append(a2); re_p.append(a3); im_p.append(a4)
        ys, b1, b2, b3, b4 = hybrid_layer(ys, state_wkv[l], state_shift[l], state_s5_re[l], state_s5_im[l], prm)
        wkv_s.append(b1); sh_s.append(b2); re_s.append(b3); im_s.append(b4)
    y_prompt = rmsnorm(yp, lnf_g)
    y_sample = rmsnorm(ys, lnf_g)
    return (y_prompt, y_sample,
            jnp.stack(wkv_p), jnp.stack(sh_p), jnp.stack(re_p), jnp.stack(im_p),
            jnp.stack(wkv_s), jnp.stack(sh_s), jnp.stack(re_s), jnp.stack(im_s))
```

```python
import functools
import math

import jax
import jax.numpy as jnp
from jax import lax
from jax.experimental import pallas as pl
from jax.experimental.pallas import tpu as pltpu

F32 = jnp.float32
BF16 = jnp.bfloat16

D_MODEL = 1024
RWKV_WIDTH = 512
HEAD_DIM = 64
RWKV_HEADS = 8
HEAD_PAIRS = RWKV_HEADS // 2
PAIR_LANES = 2 * HEAD_DIM
LORA_WIDTH = 256
RWKV_PROJ = 3 * RWKV_WIDTH + LORA_WIDTH
S5_WIDTH = 512
S5_CH = 16
S5_GROUPS = 32
S5_STATE = 64
S5_CHUNK = 16
IN_PROJ = RWKV_PROJ + S5_WIDTH
PEER_HEADS = 8
N_KEYS = 128
N_EXPERTS = N_KEYS * N_KEYS
PEER_TOPK = 16
PEER_QDIM = 256
PEER_HALF = 128
NORM_EPS = 1e-6
GN_EPS = HEAD_DIM * 1e-5
WKV_CHUNK = 64

VMEM_LIMIT_BYTES = 48 * 1024 * 1024


def _params(semantics):
    return pltpu.CompilerParams(dimension_semantics=semantics, vmem_limit_bytes=VMEM_LIMIT_BYTES)


def _dot(a, b):
    return jnp.dot(a, b, preferred_element_type=F32)


def _dot_nt(a, b):
    return lax.dot_general(a, b, (((1,), (1,)), ((), ())), preferred_element_type=F32)


def _dot_tn(a, b):
    return lax.dot_general(a, b, (((0,), (0,)), ((), ())), preferred_element_type=F32)


def _split2(x):
    hi = x.astype(BF16)
    lo = (x - hi.astype(F32)).astype(BF16)
    return hi, lo


def _split3(x):
    hi = x.astype(BF16)
    r1 = x - hi.astype(F32)
    mid = r1.astype(BF16)
    lo = (r1 - mid.astype(F32)).astype(BF16)
    return hi, mid, lo


def _head_sum(x, ones_bd):
    hi, lo = _split2(x)
    return _dot(hi, ones_bd) + _dot(lo, ones_bd)


def _rmsnorm(x, g):
    return x * lax.rsqrt(jnp.mean(x * x, axis=-1, keepdims=True) + NORM_EPS) * g


def _gelu(x):
    c = math.sqrt(2.0 / math.pi)
    return 0.5 * x * (1.0 + jnp.tanh(c * (x + 0.044715 * (x * x * x))))


def _sigmoid(x):
    return 1.0 / (1.0 + jnp.exp(-x))


def _in_proj_kernel(x_ref, g_ref, w_ref, prw_ref, u_ref):
    h = _rmsnorm(x_ref[...], g_ref[...]).astype(BF16)
    p = _dot(h, w_ref[...])
    prw_ref[...] = p[:, :RWKV_PROJ]
    u_ref[...] = p[:, RWKV_PROJ:]


def _in_proj(x2d, ln1_g, w_in_bf, tm):
    n = x2d.shape[0]
    return pl.pallas_call(
        _in_proj_kernel,
        grid=(n // tm,),
        in_specs=[
            pl.BlockSpec((tm, D_MODEL), lambda i: (i, 0)),
            pl.BlockSpec((1, D_MODEL), lambda i: (0, 0)),
            pl.BlockSpec((D_MODEL, IN_PROJ), lambda i: (0, 0)),
        ],
        out_specs=[
            pl.BlockSpec((tm, RWKV_PROJ), lambda i: (i, 0)),
            pl.BlockSpec((tm, S5_WIDTH), lambda i: (i, 0)),
        ],
        out_shape=[
            jax.ShapeDtypeStruct((n, RWKV_PROJ), F32),
            jax.ShapeDtypeStruct((n, S5_WIDTH), F32),
        ],
        compiler_params=_params(("parallel",)),
        name="in_proj",
    )(x2d, ln1_g, w_in_bf)


def _rwkv_prep_kernel(p_ref, bound_ref, mu_ref, w0_ref, a0_ref, lora_ref, kk_ref_, ka_ref, rk_ref,
                      ones_ref, r_out, lw_out, k_out, v_out, kk_out, kka_out, g_out, bonus_out):
    p = p_ref[...]
    tm = p.shape[0]
    row = lax.broadcasted_iota(jnp.int32, p.shape, 0)
    p_prev = jnp.where(row == 0, bound_ref[0], pltpu.roll(p, 1, axis=0))
    m = p + (p_prev - p) * mu_ref[...]
    r = m[:, :RWKV_WIDTH]
    k = m[:, RWKV_WIDTH:2 * RWKV_WIDTH]
    v = m[:, 2 * RWKV_WIDTH:3 * RWKV_WIDTH]
    z = m[:, 3 * RWKV_WIDTH:]
    lane = lax.broadcasted_iota(jnp.int32, (tm, LORA_WIDTH), 1)
    feat = jnp.where(lane < 64, jnp.tanh(z), jnp.where(lane < 128, z, _sigmoid(z)))
    lo = _dot(feat.astype(BF16), lora_ref[...])
    wl = -(w0_ref[...] + lo[:, :RWKV_WIDTH])
    softplus = jnp.maximum(wl, 0.0) + jnp.log(1.0 + jnp.exp(-jnp.abs(wl)))
    lw_out[...] = -jnp.exp(-softplus - 0.5)
    a = _sigmoid(a0_ref[...] + lo[:, RWKV_WIDTH:2 * RWKV_WIDTH])
    g_out[...] = lo[:, 2 * RWKV_WIDTH:]
    ones_bd = ones_ref[...]
    kk = k * kk_ref_[...]
    nrm = jnp.sqrt(_head_sum(kk * kk, ones_bd))
    kk = kk / jnp.maximum(nrm, 1e-12)
    k2 = k * (1.0 + (a - 1.0) * ka_ref[...])
    r_out[...] = r
    k_out[...] = k2
    v_out[...] = v
    kk_out[...] = kk
    kka_out[...] = kk * a
    bonus_out[...] = _head_sum(r * k2 * rk_ref[...], ones_bd) * v


def _rwkv_prep(prw, bound, prm, tm):
    n = prw.shape[0]
    row = lambda width: pl.BlockSpec((1, width), lambda i: (0, 0))
    tile = pl.BlockSpec((tm, RWKV_WIDTH), lambda i: (i, 0))
    return pl.pallas_call(
        _rwkv_prep_kernel,
        grid=(n // tm,),
        in_specs=[
            pl.BlockSpec((tm, RWKV_PROJ), lambda i: (i, 0)),
            pl.BlockSpec((1, 1, RWKV_PROJ), lambda i: (i, 0, 0)),
            row(RWKV_PROJ), row(RWKV_WIDTH), row(RWKV_WIDTH),
            pl.BlockSpec((LORA_WIDTH, 3 * RWKV_WIDTH), lambda i: (0, 0)),
            row(RWKV_WIDTH), row(RWKV_WIDTH), row(RWKV_WIDTH),
            pl.BlockSpec((RWKV_WIDTH, RWKV_WIDTH), lambda i: (0, 0)),
        ],
        out_specs=[tile] * 8,
        out_shape=[jax.ShapeDtypeStruct((n, RWKV_WIDTH), F32)] * 8,
        compiler_params=_params(("parallel",)),
        name="rwkv_prep",
    )(prw, bound, prm["mu"], prm["w0"], prm["a0"], prm["lora"], prm["k_k"], prm["k_a"], prm["r_k"],
      prm["ones_bd"])


def _wkv_pair(r, lw, k, v, kk, kka, s_bd, tri, bd_strict, bd_incl, eye2, eye_s, lane_lo):
    seq = r.shape[0]
    hi, mid, lo = _split3(lw)
    cl = _dot(tri, hi) + _dot(tri, mid) + _dot(tri, lo)
    gam = jnp.exp(cl)
    gam_prev = jnp.exp(cl - lw)
    gam_inv = jnp.exp(-cl)
    gam_last = gam[seq - 1:seq, :]
    a_t = -kk * gam_prev
    r_t = r * gam
    b_t = kka * gam_inv
    k_t = k * gam_inv
    b_h = b_t * gam_last
    k_h = k_t * gam_last

    def stack(z):
        return jnp.concatenate([jnp.where(lane_lo, z, 0.0), jnp.where(lane_lo, 0.0, z)], axis=0)

    a_st, r_st, v_st = stack(a_t), stack(r_t), stack(v)
    bh_st, kh_st = stack(b_h), stack(k_h)
    lhs = jnp.concatenate([a_st, r_st], axis=0).astype(BF16)
    rhs = jnp.concatenate([b_t, b_t, k_t, k_t], axis=0).astype(BF16)
    x = _dot_nt(lhs, rhs)
    two = 2 * seq
    m_ab = jnp.where(bd_strict, x[:two, :two], 0.0)
    m_ak = jnp.where(bd_strict, x[:two, two:], 0.0)
    n_rb = jnp.where(bd_incl, x[two:, :two], 0.0)
    n_rk = jnp.where(bd_incl, x[two:, two:], 0.0)
    t_inv = eye2 + m_ab
    pw = m_ab
    steps = 1
    while steps * 2 < seq:
        pw_b = pw.astype(BF16)
        pw = _dot(pw_b, pw_b)
        t_inv = _dot(t_inv.astype(BF16), (eye2 + pw).astype(BF16))
        steps *= 2
    v_st_b = v_st.astype(BF16)
    y = jnp.concatenate([a_st, _dot(m_ak.astype(BF16), v_st_b)], axis=1)
    ty = _dot(t_inv.astype(BF16), y.astype(BF16))
    ah_st = ty[:, :PAIR_LANES]
    uh_st = ty[:, PAIR_LANES:]
    ah_b = ah_st.astype(BF16)
    uh_b = uh_st.astype(BF16)
    p_t = _dot_tn(bh_st.astype(BF16), ah_b) + jnp.where(eye_s, gam_last, 0.0)
    q_t = _dot_tn(jnp.concatenate([bh_st, kh_st], axis=0).astype(BF16),
                  jnp.concatenate([uh_b, v_st_b], axis=0))
    n_cat = jnp.concatenate([n_rb, n_rk], axis=1).astype(BF16)
    rhs2 = jnp.concatenate(
        [jnp.concatenate([ah_b, uh_b], axis=1),
         jnp.concatenate([jnp.zeros_like(v_st_b), v_st_b], axis=1)], axis=0)
    ro = _dot(n_cat, rhs2)
    rh_st = r_st + ro[:, :PAIR_LANES]
    oh_st = ro[:, PAIR_LANES:]
    s_hi, s_lo = _split2(s_bd)
    rh_hi, rh_lo = _split2(rh_st)
    o_st = _dot(rh_hi, s_hi) + _dot(rh_hi, s_lo) + _dot(rh_lo, s_hi) + oh_st
    o = o_st[:seq] + o_st[seq:]
    p_hi, p_lo = _split2(p_t)
    s_new = _dot(p_hi, s_hi) + _dot(p_hi, s_lo) + _dot(p_lo, s_hi) + q_t
    return o, s_new


def _wkv_kernel(r_ref, lw_ref, k_ref, v_ref, kk_ref, kka_ref, s0_ref, o_ref, s1_ref, s_scr):
    j = pl.program_id(1)
    seq = r_ref.shape[0]
    two = 2 * seq

    @pl.when(j == 0)
    def _():
        s_scr[...] = s0_ref[...]

    ri = lax.broadcasted_iota(jnp.int32, (seq, seq), 0)
    ci = lax.broadcasted_iota(jnp.int32, (seq, seq), 1)
    tri = (ri >= ci).astype(BF16)
    r2 = lax.broadcasted_iota(jnp.int32, (two, two), 0)
    c2 = lax.broadcasted_iota(jnp.int32, (two, two), 1)
    same = (r2 // seq) == (c2 // seq)
    bd_strict = same & ((r2 % seq) > (c2 % seq))
    bd_incl = same & ((r2 % seq) >= (c2 % seq))
    eye2 = (r2 == c2).astype(F32)
    rs = lax.broadcasted_iota(jnp.int32, (PAIR_LANES, PAIR_LANES), 0)
    cs = lax.broadcasted_iota(jnp.int32, (PAIR_LANES, PAIR_LANES), 1)
    eye_s = rs == cs
    lane_lo = lax.broadcasted_iota(jnp.int32, (seq, PAIR_LANES), 1) < HEAD_DIM

    for pair in range(HEAD_PAIRS):
        sl = slice(pair * PAIR_LANES, (pair + 1) * PAIR_LANES)
        o, s_new = _wkv_pair(r_ref[:, sl], lw_ref[:, sl], k_ref[:, sl], v_ref[:, sl], kk_ref[:, sl],
                             kka_ref[:, sl], s_scr[pair], tri, bd_strict, bd_incl, eye2, eye_s, lane_lo)
        o_ref[:, sl] = o
        s_scr[pair] = s_new

    @pl.when(j == pl.num_programs(1) - 1)
    def _():
        s1_ref[...] = s_scr[...]


def _wkv(r, lw, k, v, kk, kka, s0_bd):
    bsz, t_len, _ = r.shape
    nc = t_len // WKV_CHUNK
    tile = pl.BlockSpec((None, WKV_CHUNK, RWKV_WIDTH), lambda b, j: (b, j, 0))
    st = pl.BlockSpec((None, HEAD_PAIRS, PAIR_LANES, PAIR_LANES), lambda b, j: (b, 0, 0, 0))
    return pl.pallas_call(
        _wkv_kernel,
        grid=(bsz, nc),
        in_specs=[tile] * 6 + [st],
        out_specs=[tile, st],
        out_shape=[
            jax.ShapeDtypeStruct((bsz, t_len, RWKV_WIDTH), F32),
            jax.ShapeDtypeStruct((bsz, HEAD_PAIRS, PAIR_LANES, PAIR_LANES), F32),
        ],
        scratch_shapes=[pltpu.VMEM((HEAD_PAIRS, PAIR_LANES, PAIR_LANES), F32)],
        compiler_params=_params(("parallel", "arbitrary")),
        name="wkv",
    )(r, lw, k, v, kk, kka, s0_bd)


def _s5_kernel(u_ref, x0re_ref, x0im_ref, wst_ref, wy_ref, tmat_ref, dre_ref, dim_ref,
               y_ref, x1re_ref, x1im_ref, zre_scr, zim_scr, *, bsz, nc):
    half = 2 * S5_STATE
    u2 = jnp.concatenate([u_ref[0], u_ref[1]], axis=1).astype(BF16)
    z = _dot(u2, wst_ref[0])
    zre_scr[...] = z[:, :half]
    zim_scr[...] = z[:, half:]
    dre = dre_ref[0]
    dim = dim_ref[0]

    def body(j, carry):
        new = []
        for b in range(bsz):
            xre, xim = carry[2 * b], carry[2 * b + 1]
            row = pl.ds(b * nc + j, 1)
            zre = zre_scr[row, :]
            zim = zim_scr[row, :]
            zre_scr[row, :] = xre
            zim_scr[row, :] = xim
            new.append(dre * xre - dim * xim + zre)
            new.append(dre * xim + dim * xre + zim)
        return tuple(new)

    init = []
    for b in range(bsz):
        init.append(x0re_ref[0, b:b + 1, :])
        init.append(x0im_ref[0, b:b + 1, :])
    fin = lax.fori_loop(0, nc, body, tuple(init))
    for b in range(bsz):
        x1re_ref[0, b:b + 1, :] = fin[2 * b]
        x1im_ref[0, b:b + 1, :] = fin[2 * b + 1]
    xin = jnp.concatenate([zre_scr[...], zim_scr[...]], axis=1).astype(BF16)
    y = _dot(u2, tmat_ref[0]) + _dot(xin, wy_ref[0])
    width = S5_CHUNK * S5_CH
    y_ref[0] = y[:, :width]
    y_ref[1] = y[:, width:]


def _s5(uf, x0re, x0im, mats, bsz, nc):
    rows = bsz * nc
    gp = S5_GROUPS // 2
    width = S5_CHUNK * S5_CH
    half = 2 * S5_STATE
    st = pl.BlockSpec((1, bsz, half), lambda g: (g, 0, 0))
    dec = pl.BlockSpec((1, 1, half), lambda g: (g, 0, 0))
    return pl.pallas_call(
        functools.partial(_s5_kernel, bsz=bsz, nc=nc),
        grid=(gp,),
        in_specs=[
            pl.BlockSpec((2, rows, width), lambda g: (g, 0, 0)),
            st, st,
            pl.BlockSpec((1, 2 * width, 2 * half), lambda g: (g, 0, 0)),
            pl.BlockSpec((1, 2 * half, 2 * width), lambda g: (g, 0, 0)),
            pl.BlockSpec((1, 2 * width, 2 * width), lambda g: (g, 0, 0)),
            dec, dec,
        ],
        out_specs=[pl.BlockSpec((2, rows, width), lambda g: (g, 0, 0)), st, st],
        out_shape=[
            jax.ShapeDtypeStruct((S5_GROUPS, rows, width), F32),
            jax.ShapeDtypeStruct((gp, bsz, half), F32),
            jax.ShapeDtypeStruct((gp, bsz, half), F32),
        ],
        scratch_shapes=[pltpu.VMEM((rows, half), F32), pltpu.VMEM((rows, half), F32)],
        compiler_params=_params(("parallel",)),
        name="s5",
    )(uf, x0re, x0im, mats["wst"], mats["wy"], mats["tmat"], mats["dre"], mats["dim"])


def _s5_matrices(a_re, a_im, log_dt, b_re, b_im, c_re, c_im, d):
    L = S5_CHUNK
    dt = jnp.exp(log_dt)[:, None]
    n = jnp.arange(L + 1, dtype=F32)[:, None, None]
    mag = jnp.exp(a_re * dt * n)
    ang = a_im * dt * n
    pw_re, pw_im = mag * jnp.cos(ang), mag * jnp.sin(ang)
    num_re, num_im = pw_re[1] - 1.0, pw_im[1]
    den = a_re * a_re + a_im * a_im
    f_re = (num_re * a_re + num_im * a_im) / den
    f_im = (num_im * a_re - num_re * a_im) / den
    bb_re = f_re[..., None] * b_re - f_im[..., None] * b_im
    bb_im = f_re[..., None] * b_im + f_im[..., None] * b_re
    pb_re = pw_re[..., None] * bb_re - pw_im[..., None] * bb_im
    pb_im = pw_re[..., None] * bb_im + pw_im[..., None] * bb_re
    kern = (jnp.einsum('gcp,ngpd->ngcd', c_re, pb_re[:L]) - jnp.einsum('gcp,ngpd->ngcd', c_im, pb_im[:L]))
    s_idx = jnp.arange(L)[:, None]
    t_idx = jnp.arange(L)[None, :]
    lag = jnp.clip(t_idx - s_idx, 0, L - 1)
    tk = kern[lag]
    tk = jnp.where((t_idx >= s_idx)[:, :, None, None, None], tk, 0.0)
    tmat = tk.transpose(2, 0, 4, 1, 3)
    eye_st = jnp.eye(L, dtype=F32)[:, None, :, None] * jnp.eye(S5_CH, dtype=F32)[None, :, None, :]
    tmat = tmat + d.reshape(S5_GROUPS, 1, 1, 1, S5_CH) * eye_st[None]
    tmat = tmat.reshape(S5_GROUPS, L * S5_CH, L * S5_CH)
    inj_re = pb_re[:L][::-1].transpose(1, 0, 3, 2).reshape(S5_GROUPS, L * S5_CH, S5_STATE)
    inj_im = pb_im[:L][::-1].transpose(1, 0, 3, 2).reshape(S5_GROUPS, L * S5_CH, S5_STATE)
    cp_re = (c_re[None] * pw_re[1:, :, None, :] - c_im[None] * pw_im[1:, :, None, :])
    cp_im = (c_re[None] * pw_im[1:, :, None, :] + c_im[None] * pw_re[1:, :, None, :])
    rd_re = cp_re.transpose(1, 3, 0, 2).reshape(S5_GROUPS, S5_STATE, L * S5_CH)
    rd_im = -cp_im.transpose(1, 3, 0, 2).reshape(S5_GROUPS, S5_STATE, L * S5_CH)

    gp = S5_GROUPS // 2
    width = L * S5_CH
    z = jnp.zeros((gp, width, S5_STATE), F32)
    pair = lambda m: m.reshape(gp, 2, *m.shape[1:])
    ir, ii = pair(inj_re), pair(inj_im)
    wst = jnp.concatenate([
        jnp.concatenate([ir[:, 0], z, ii[:, 0], z], axis=2),
        jnp.concatenate([z, ir[:, 1], z, ii[:, 1]], axis=2)], axis=1)
    zr = jnp.zeros((gp, S5_STATE, width), F32)
    rr, ri = pair(rd_re), pair(rd_im)
    wy = jnp.concatenate([
        jnp.concatenate([rr[:, 0], zr], axis=2),
        jnp.concatenate([zr, rr[:, 1]], axis=2),
        jnp.concatenate([ri[:, 0], zr], axis=2),
        jnp.concatenate([zr, ri[:, 1]], axis=2)], axis=1)
    tp = pair(tmat)
    zt = jnp.zeros((gp, width, width), F32)
    tm2 = jnp.concatenate([
        jnp.concatenate([tp[:, 0], zt], axis=2),
        jnp.concatenate([zt, tp[:, 1]], axis=2)], axis=1)
    dre = pw_re[L].reshape(gp, 1, 2 * S5_STATE)
    dim = pw_im[L].reshape(gp, 1, 2 * S5_STATE)
    return {"wst": wst.astype(BF16), "wy": wy.astype(BF16), "tmat": tm2.astype(BF16),
            "dre": dre, "dim": dim}


def _mix_out_kernel(o_ref, bonus_ref, g_ref, y_ref, x_ref, gnw_ref, gnb_ref, wglu_ref, bglu_ref,
                    wout_ref, ones_ref, x1_ref):
    ones_bd = ones_ref[...]
    o = o_ref[...]
    mu = _head_sum(o, ones_bd) * (1.0 / HEAD_DIM)
    d = o - mu
    var = _head_sum(d * d, ones_bd) * (1.0 / HEAD_DIM)
    o = d * lax.rsqrt(var + GN_EPS) * gnw_ref[...] + gnb_ref[...]
    o = (o + bonus_ref[...]) * g_ref[...]
    y = _gelu(y_ref[...])
    y = y * _sigmoid(_dot(y.astype(BF16), wglu_ref[...]) + bglu_ref[...])
    mixed = jnp.concatenate([o, y], axis=1).astype(BF16)
    x1_ref[...] = x_ref[...] + _dot(mixed, wout_ref[...])


def _mix_out(o, bonus, g, y, x2d, prm, tm):
    n = x2d.shape[0]
    tile = pl.BlockSpec((tm, RWKV_WIDTH), lambda i: (i, 0))
    row = pl.BlockSpec((1, RWKV_WIDTH), lambda i: (0, 0))
    sq = pl.BlockSpec((RWKV_WIDTH, RWKV_WIDTH), lambda i: (0, 0))
    return pl.pallas_call(
        _mix_out_kernel,
        grid=(n // tm,),
        in_specs=[tile, tile, tile, tile, pl.BlockSpec((tm, D_MODEL), lambda i: (i, 0)),
                  row, row, sq, row, pl.BlockSpec((D_MODEL, D_MODEL), lambda i: (0, 0)), sq],
        out_specs=pl.BlockSpec((tm, D_MODEL), lambda i: (i, 0)),
        out_shape=jax.ShapeDtypeStruct((n, D_MODEL), F32),
        compiler_params=_params(("parallel",)),
        name="mix_out",
    )(o, bonus, g, y, x2d, prm["gn_w"], prm["gn_b"], prm["w_glu"], prm["b_glu"], prm["w_out"],
      prm["ones_bd"])


def _top16(x):
    out = []
    for _ in range(PEER_TOPK):
        m = jnp.max(x, axis=0, keepdims=True)
        out.append(m)
        x = jnp.where(x == m, -jnp.inf, x)
    return out


def _peer_route_kernel(x_ref, g_ref, wqt_ref, k1_ref, k2_ref,
                       h2t_ref, s1_ref, e1_ref, s2_ref, e2_ref, thr_ref):
    h2 = _rmsnorm(x_ref[...], g_ref[...])
    h2_b = h2.astype(BF16)
    h2t_ref[...] = jnp.transpose(h2).astype(BF16)
    qt = _dot_nt(wqt_ref[...], h2_b)
    for h in range(PEER_HEADS):
        q1 = qt[h * PEER_QDIM:h * PEER_QDIM + PEER_HALF].astype(BF16)
        q2 = qt[h * PEER_QDIM + PEER_HALF:(h + 1) * PEER_QDIM].astype(BF16)
        s1 = _dot(k1_ref[h], q1)
        s2 = _dot(k2_ref[h], q2)
        a1 = _top16(s1)
        a2 = _top16(s2)
        a2_lo = jnp.concatenate(a2[:8], axis=0)
        cand = [a1[0] + a2_lo, a1[0] + jnp.concatenate(a2[8:], axis=0)]
        cand += [a1[i] + a2_lo for i in range(1, 8)]
        cand.append(jnp.concatenate(a1[8:], axis=0) + a2[0])
        cand = jnp.concatenate(cand, axis=0)
        top = a1[0] + a2[0]
        zsum = jnp.zeros_like(top)
        thr = top
        for _ in range(PEER_TOPK):
            thr = jnp.max(cand, axis=0, keepdims=True)
            zsum = zsum + jnp.exp(thr - top)
            cand = jnp.where(cand == thr, -jnp.inf, cand)
        s1_ref[h] = s1
        s2_ref[h] = s2
        e1_ref[h] = jnp.exp(s1 - a1[0])
        e2_ref[h] = jnp.exp(s2 - a2[0]) / zsum
        thr_ref[h] = thr


def _peer_route(x1, ln2_g, wqt_bf, k1_bf, k2_bf, tt):
    n = x1.shape[0]
    keys = pl.BlockSpec((PEER_HEADS, N_KEYS, PEER_HALF), lambda i: (0, 0, 0))
    tok = pl.BlockSpec((PEER_HEADS, N_KEYS, tt), lambda i: (0, 0, i))
    return pl.pallas_call(
        _peer_route_kernel,
        grid=(n // tt,),
        in_specs=[
            pl.BlockSpec((tt, D_MODEL), lambda i: (i, 0)),
            pl.BlockSpec((1, D_MODEL), lambda i: (0, 0)),
            pl.BlockSpec((PEER_HEADS * PEER_QDIM, D_MODEL), lambda i: (0, 0)),
            keys, keys,
        ],
        out_specs=[pl.BlockSpec((D_MODEL, tt), lambda i: (0, i)), tok, tok, tok, tok,
                   pl.BlockSpec((PEER_HEADS, 1, tt), lambda i: (0, 0, i))],
        out_shape=[jax.ShapeDtypeStruct((D_MODEL, n), BF16)]
        + [jax.ShapeDtypeStruct((PEER_HEADS, N_KEYS, n), F32)] * 4
        + [jax.ShapeDtypeStruct((PEER_HEADS, 1, n), F32)],
        compiler_params=_params(("parallel",)),
        name="peer_route",
    )(x1, ln2_g, wqt_bf, k1_bf, k2_bf)


def _peer_dense_kernel(h2t_ref, u_ref, vt_ref, s1_ref, e1_ref, s2_ref, e2_ref, thr_ref, x1_ref, g_ref,
                       y_ref, acc_scr, a_scr, w_scr):
    c = pl.program_id(1)
    rows_per_chunk = s1_ref.shape[1]

    @pl.when(c == 0)
    def _():
        acc_scr[...] = jnp.zeros_like(acc_scr)

    a_scr[...] = _dot(u_ref[...], h2t_ref[...])

    def build(i1, carry):
        base = pl.multiple_of(i1 * N_KEYS, N_KEYS)
        gate = jnp.zeros((N_KEYS, a_scr.shape[1]), F32)
        for h in range(PEER_HEADS):
            score = s2_ref[h] + s1_ref[h, pl.ds(i1, 1), :]
            sel = jnp.where(score >= thr_ref[h], e2_ref[h], 0.0)
            gate = gate + sel * e1_ref[h, pl.ds(i1, 1), :]
        w_scr[pl.ds(base, N_KEYS), :] = (_gelu(a_scr[pl.ds(base, N_KEYS), :]) * gate).astype(BF16)
        return carry

    lax.fori_loop(0, rows_per_chunk, build, 0)
    acc_scr[...] += _dot(vt_ref[...], w_scr[...])

    @pl.when(c == pl.num_programs(1) - 1)
    def _():
        x2 = x1_ref[...] + jnp.transpose(acc_scr[...])
        y_ref[...] = _rmsnorm(x2, g_ref[...])


def _peer_dense(h2t, u_bf, vt_bf, s1, e1, s2, e2, thr, x1, lnf_g, tt, ec):
    n = x1.shape[0]
    i1c = ec // N_KEYS
    tok_all = pl.BlockSpec((PEER_HEADS, N_KEYS, tt), lambda i, c: (0, 0, i))
    tok_chunk = pl.BlockSpec((PEER_HEADS, i1c, tt), lambda i, c: (0, c, i))
    return pl.pallas_call(
        _peer_dense_kernel,
        grid=(n // tt, N_EXPERTS // ec),
        in_specs=[
            pl.BlockSpec((D_MODEL, tt), lambda i, c: (0, i)),
            pl.BlockSpec((ec, D_MODEL), lambda i, c: (c, 0)),
            pl.BlockSpec((D_MODEL, ec), lambda i, c: (0, c)),
            tok_chunk, tok_chunk, tok_all, tok_all,
            pl.BlockSpec((PEER_HEADS, 1, tt), lambda i, c: (0, 0, i)),
            pl.BlockSpec((tt, D_MODEL), lambda i, c: (i, 0)),
            pl.BlockSpec((1, D_MODEL), lambda i, c: (0, 0)),
        ],
        out_specs=pl.BlockSpec((tt, D_MODEL), lambda i, c: (i, 0)),
        out_shape=jax.ShapeDtypeStruct((n, D_MODEL), F32),
        scratch_shapes=[pltpu.VMEM((D_MODEL, tt), F32), pltpu.VMEM((ec, tt), F32),
                        pltpu.VMEM((ec, tt), BF16)],
        compiler_params=_params(("parallel", "arbitrary")),
        name="peer_dense",
    )(h2t, u_bf, vt_bf, s1, e1, s2, e2, thr, x1, lnf_g)


def _pick_tile(n, target):
    t = min(n, target)
    while n % t:
        t //= 2
    return t


def _layer(x, wkv0, shift0, s5re0, s5im0, prm, s5m, lnf_g):
    bsz, t_len, _ = x.shape
    n = bsz * t_len
    x2d = x.reshape(n, D_MODEL)
    tm = _pick_tile(t_len, 512)
    prw, u = _in_proj(x2d, prm["ln1_g"], prm["w_in"], tm)

    tiles_per_seq = t_len // tm
    prw3 = prw.reshape(bsz, t_len, RWKV_PROJ)
    last_rows = prw3[:, tm - 1::tm, :]
    bound = jnp.concatenate([shift0, last_rows[:, :tiles_per_seq - 1, :]], axis=1)
    bound = bound.reshape(bsz * tiles_per_seq, 1, RWKV_PROJ)
    r, lw, k, v, kk, kka, g, bonus = _rwkv_prep(prw, bound, prm, tm)

    t_pad = -(-t_len // WKV_CHUNK) * WKV_CHUNK
    seq3 = lambda z: jnp.pad(z.reshape(bsz, t_len, RWKV_WIDTH), ((0, 0), (0, t_pad - t_len), (0, 0)))
    st = wkv0.reshape(bsz, HEAD_PAIRS, 2, HEAD_DIM, HEAD_DIM).transpose(0, 1, 2, 4, 3)
    eye2 = jnp.eye(2, dtype=F32)
    s0_bd = jnp.einsum('bphkv,hg->bphkgv', st, eye2).reshape(bsz, HEAD_PAIRS, PAIR_LANES, PAIR_LANES)
    o, s1_bd = _wkv(seq3(r), seq3(lw), seq3(k), seq3(v), seq3(kk), seq3(kka), s0_bd)
    o = o[:, :t_len].reshape(n, RWKV_WIDTH)
    s1 = s1_bd.reshape(bsz, HEAD_PAIRS, 2, HEAD_DIM, 2, HEAD_DIM)
    s1 = jnp.stack([s1[:, :, 0, :, 0, :], s1[:, :, 1, :, 1, :]], axis=2)
    wkv1 = s1.transpose(0, 1, 2, 4, 3).reshape(bsz, RWKV_HEADS, HEAD_DIM, HEAD_DIM)

    nc = t_len // S5_CHUNK
    uf = u.reshape(bsz, nc, S5_CHUNK, S5_GROUPS, S5_CH).transpose(3, 0, 1, 2, 4)
    uf = uf.reshape(S5_GROUPS, bsz * nc, S5_CHUNK * S5_CH)
    gp = S5_GROUPS // 2
    pk = lambda z: z.reshape(bsz, gp, 2 * S5_STATE).transpose(1, 0, 2)
    yf, x1re, x1im = _s5(uf, pk(s5re0), pk(s5im0), s5m, bsz, nc)
    y = yf.reshape(S5_GROUPS, bsz, nc, S5_CHUNK, S5_CH).transpose(1, 2, 3, 0, 4).reshape(n, S5_WIDTH)
    unpk = lambda z: z.transpose(1, 0, 2).reshape(bsz, S5_GROUPS, S5_STATE)

    x1 = _mix_out(o, bonus, g, y, x2d, prm, tm)

    tt = _pick_tile(n, 512)
    h2t, ps1, pe1, ps2, pe2, thr = _peer_route(x1, prm["ln2_g"], prm["w_qt"], prm["keys1"], prm["keys2"], tt)
    yout = _peer_dense(h2t, prm["peer_u"], prm["peer_vt"], ps1, pe1, ps2, pe2, thr, x1, lnf_g, tt, 1024)
    shift1 = prw3[:, -1:, :]
    return yout.reshape(bsz, t_len, D_MODEL), wkv1, shift1, unpk(x1re), unpk(x1im)


def kernel(x_prompt, x_sample, state_wkv, state_shift, state_s5_re, state_s5_im, ln1_g, w_in, rwkv_mu, rwkv_w0, rwkv_w2, rwkv_a0, rwkv_a2, rwkv_g2, rwkv_k_k, rwkv_k_a, rwkv_r_k, rwkv_gn_w, rwkv_gn_b, s5_a_re, s5_a_im, s5_log_dt, s5_b_re, s5_b_im, s5_c_re, s5_c_im, s5_d, s5_w_glu, s5_b_glu, w_out, ln2_g, peer_w_q, peer_keys1, peer_keys2, peer_u, peer_v, lnf_g):
    depth = w_in.shape[0]
    assert depth == 1
    l = 0
    row = lambda z: z.reshape(1, -1).astype(F32)
    lora = jnp.zeros((LORA_WIDTH, 3 * RWKV_WIDTH), F32)
    lora = lora.at[:64, :RWKV_WIDTH].set(rwkv_w2[l])
    lora = lora.at[64:128, RWKV_WIDTH:2 * RWKV_WIDTH].set(rwkv_a2[l])
    lora = lora.at[128:, 2 * RWKV_WIDTH:].set(rwkv_g2[l])
    lane = jnp.arange(RWKV_WIDTH)
    ones_bd = (lane[:, None] // HEAD_DIM == lane[None, :] // HEAD_DIM).astype(BF16)
    prm = {
        "ln1_g": row(ln1_g[l]), "w_in": w_in[l].astype(BF16), "mu": row(rwkv_mu[l]),
        "w0": row(rwkv_w0[l]), "a0": row(rwkv_a0[l]), "lora": lora.astype(BF16),
        "k_k": row(rwkv_k_k[l]), "k_a": row(rwkv_k_a[l]), "r_k": row(rwkv_r_k[l]),
        "gn_w": row(rwkv_gn_w[l]), "gn_b": row(rwkv_gn_b[l]), "ones_bd": ones_bd,
        "w_glu": s5_w_glu[l].astype(BF16), "b_glu": row(s5_b_glu[l]), "w_out": w_out[l].astype(BF16),
        "ln2_g": row(ln2_g[l]), "w_qt": peer_w_q[l].T.astype(BF16),
        "keys1": peer_keys1[l].astype(BF16), "keys2": peer_keys2[l].astype(BF16),
        "peer_u": peer_u[l].astype(BF16), "peer_vt": peer_v[l].T.astype(BF16),
    }
    s5m = _s5_matrices(s5_a_re[l], s5_a_im[l], s5_log_dt[l], s5_b_re[l], s5_b_im[l], s5_c_re[l],
                       s5_c_im[l], s5_d[l])
    lnf = row(lnf_g)
    bp = x_prompt.shape[0]
    zeros = lambda *s: jnp.zeros(s, F32)
    yp, a1, a2, a3, a4 = _layer(
        x_prompt, zeros(bp, RWKV_HEADS, HEAD_DIM, HEAD_DIM), zeros(bp, 1, RWKV_PROJ),
        zeros(bp, S5_GROUPS, S5_STATE), zeros(bp, S5_GROUPS, S5_STATE), prm, s5m, lnf)
    ys, b1, b2, b3, b4 = _layer(x_sample, state_wkv[l], state_shift[l], state_s5_re[l], state_s5_im[l],
                                prm, s5m, lnf)
    st = lambda z: z[None]
    return (yp, ys, st(a1), st(a2), st(a3), st(a4), st(b1), st(b2), st(b3), st(b4))
```

```python
import functools
import math

import jax
import jax.numpy as jnp
from jax import lax
from jax.experimental import pallas as pl
from jax.experimental.pallas import tpu as pltpu

F32 = jnp.float32
BF16 = jnp.bfloat16

D_MODEL = 1024
RWKV_WIDTH = 512
HEAD_DIM = 64
RWKV_HEADS = 8
HEAD_PAIRS = RWKV_HEADS // 2
PAIR_LANES = 2 * HEAD_DIM
LORA_WIDTH = 256
RWKV_PROJ = 3 * RWKV_WIDTH + LORA_WIDTH
S5_WIDTH = 512
S5_CH = 16
S5_GROUPS = 32
S5_STATE = 64
S5_CHUNK = 16
IN_PROJ = RWKV_PROJ + S5_WIDTH
PEER_HEADS = 8
N_KEYS = 128
N_EXPERTS = N_KEYS * N_KEYS
PEER_TOPK = 16
PEER_QDIM = 256
PEER_HALF = 128
NORM_EPS = 1e-6
GN_EPS = HEAD_DIM * 1e-5
WKV_CHUNK = 64

VMEM_LIMIT_BYTES = 48 * 1024 * 1024


def _params(semantics, flags=None):
    return pltpu.CompilerParams(dimension_semantics=semantics, vmem_limit_bytes=VMEM_LIMIT_BYTES,
                                flags=flags)


def _dot(a, b):
    return jnp.dot(a, b, preferred_element_type=F32)


def _dot_nt(a, b):
    return lax.dot_general(a, b, (((1,), (1,)), ((), ())), preferred_element_type=F32)


def _dot_tn(a, b):
    return lax.dot_general(a, b, (((0,), (0,)), ((), ())), preferred_element_type=F32)


def _split2(x):
    hi = x.astype(BF16)
    lo = (x - hi.astype(F32)).astype(BF16)
    return hi, lo


def _split3(x):
    hi = x.astype(BF16)
    r1 = x - hi.astype(F32)
    mid = r1.astype(BF16)
    lo = (r1 - mid.astype(F32)).astype(BF16)
    return hi, mid, lo


def _head_sum(x, ones_bd):
    hi, lo = _split2(x)
    return _dot(hi, ones_bd) + _dot(lo, ones_bd)


def _rmsnorm(x, g):
    return x * lax.rsqrt(jnp.mean(x * x, axis=-1, keepdims=True) + NORM_EPS) * g


def _gelu(x):
    c = math.sqrt(2.0 / math.pi)
    return 0.5 * x * (1.0 + jnp.tanh(c * (x + 0.044715 * (x * x * x))))


def _sigmoid(x):
    return 1.0 / (1.0 + jnp.exp(-x))


def _in_proj_kernel(x_ref, g_ref, w_ref, prw_ref, u_ref):
    h = _rmsnorm(x_ref[...], g_ref[...]).astype(BF16)
    p = _dot(h, w_ref[...])
    prw_ref[...] = p[:, :RWKV_PROJ]
    u_ref[...] = p[:, RWKV_PROJ:]


def _in_proj(x2d, ln1_g, w_in_bf, tm):
    n = x2d.shape[0]
    return pl.pallas_call(
        _in_proj_kernel,
        grid=(n // tm,),
        in_specs=[
            pl.BlockSpec((tm, D_MODEL), lambda i: (i, 0)),
            pl.BlockSpec((1, D_MODEL), lambda i: (0, 0)),
            pl.BlockSpec((D_MODEL, IN_PROJ), lambda i: (0, 0)),
        ],
        out_specs=[
            pl.BlockSpec((tm, RWKV_PROJ), lambda i: (i, 0)),
            pl.BlockSpec((tm, S5_WIDTH), lambda i: (i, 0)),
        ],
        out_shape=[
            jax.ShapeDtypeStruct((n, RWKV_PROJ), F32),
            jax.ShapeDtypeStruct((n, S5_WIDTH), F32),
        ],
        compiler_params=_params(("parallel",)),
        name="in_proj",
    )(x2d, ln1_g, w_in_bf)


def _rwkv_prep_kernel(p_ref, bound_ref, mu_ref, w0_ref, a0_ref, lora_ref, kk_ref_, ka_ref, rk_ref,
                      ones_ref, r_out, lw_out, k_out, v_out, kk_out, kka_out, g_out, bonus_out):
    p = p_ref[...]
    tm = p.shape[0]
    row = lax.broadcasted_iota(jnp.int32, p.shape, 0)
    p_prev = jnp.where(row == 0, bound_ref[0], pltpu.roll(p, 1, axis=0))
    m = p + (p_prev - p) * mu_ref[...]
    r = m[:, :RWKV_WIDTH]
    k = m[:, RWKV_WIDTH:2 * RWKV_WIDTH]
    v = m[:, 2 * RWKV_WIDTH:3 * RWKV_WIDTH]
    z = m[:, 3 * RWKV_WIDTH:]
    lane = lax.broadcasted_iota(jnp.int32, (tm, LORA_WIDTH), 1)
    feat = jnp.where(lane < 64, jnp.tanh(z), jnp.where(lane < 128, z, _sigmoid(z)))
    lo = _dot(feat.astype(BF16), lora_ref[...])
    wl = -(w0_ref[...] + lo[:, :RWKV_WIDTH])
    softplus = jnp.maximum(wl, 0.0) + jnp.log(1.0 + jnp.exp(-jnp.abs(wl)))
    lw_out[...] = -jnp.exp(-softplus - 0.5)
    a = _sigmoid(a0_ref[...] + lo[:, RWKV_WIDTH:2 * RWKV_WIDTH])
    g_out[...] = lo[:, 2 * RWKV_WIDTH:]
    ones_bd = ones_ref[...]
    kk = k * kk_ref_[...]
    nrm = jnp.sqrt(_head_sum(kk * kk, ones_bd))
    kk = kk / jnp.maximum(nrm, 1e-12)
    k2 = k * (1.0 + (a - 1.0) * ka_ref[...])
    r_out[...] = r
    k_out[...] = k2
    v_out[...] = v
    kk_out[...] = kk
    kka_out[...] = kk * a
    bonus_out[...] = _head_sum(r * k2 * rk_ref[...], ones_bd) * v


def _rwkv_prep(prw, bound, prm, tm):
    n = prw.shape[0]
    row = lambda width: pl.BlockSpec((1, width), lambda i: (0, 0))
    tile = pl.BlockSpec((tm, RWKV_WIDTH), lambda i: (i, 0))
    return pl.pallas_call(
        _rwkv_prep_kernel,
        grid=(n // tm,),
        in_specs=[
            pl.BlockSpec((tm, RWKV_PROJ), lambda i: (i, 0)),
            pl.BlockSpec((1, 1, RWKV_PROJ), lambda i: (i, 0, 0)),
            row(RWKV_PROJ), row(RWKV_WIDTH), row(RWKV_WIDTH),
            pl.BlockSpec((LORA_WIDTH, 3 * RWKV_WIDTH), lambda i: (0, 0)),
            row(RWKV_WIDTH), row(RWKV_WIDTH), row(RWKV_WIDTH),
            pl.BlockSpec((RWKV_WIDTH, RWKV_WIDTH), lambda i: (0, 0)),
        ],
        out_specs=[tile] * 8,
        out_shape=[jax.ShapeDtypeStruct((n, RWKV_WIDTH), F32)] * 8,
        compiler_params=_params(("parallel",)),
        name="rwkv_prep",
    )(prw, bound, prm["mu"], prm["w0"], prm["a0"], prm["lora"], prm["k_k"], prm["k_a"], prm["r_k"],
      prm["ones_bd"])


def _wkv_units(units, s_list, tri, bd_strict, bd_incl, eye2, eye_s, lane_lo):
    seq = units[0][0].shape[0]
    two = 2 * seq
    n = len(units)
    rng = range(n)

    def stack(z):
        return jnp.concatenate([jnp.where(lane_lo, z, 0.0), jnp.where(lane_lo, 0.0, z)], axis=0)

    cl = []
    for (r, lw, k, v, kk, kka) in units:
        hi, mid, lo = _split3(lw)
        cl.append(_dot(tri, hi) + _dot(tri, mid) + _dot(tri, lo))
    gam = [jnp.exp(c) for c in cl]
    gam_inv = [jnp.exp(-c) for c in cl]
    gam_prev = [jnp.exp(cl[i] - units[i][1]) for i in rng]
    gam_last = [g[seq - 1:seq, :] for g in gam]
    a_t = [-units[i][4] * gam_prev[i] for i in rng]
    r_t = [units[i][0] * gam[i] for i in rng]
    b_t = [units[i][5] * gam_inv[i] for i in rng]
    k_t = [units[i][2] * gam_inv[i] for i in rng]
    a_st = [stack(z) for z in a_t]
    r_st = [stack(z) for z in r_t]
    v_st_b = [stack(u[3]).astype(BF16) for u in units]
    bh_st = [stack(b_t[i] * gam_last[i]).astype(BF16) for i in rng]
    kh_st = [stack(k_t[i] * gam_last[i]).astype(BF16) for i in rng]
    x = [_dot_nt(jnp.concatenate([a_st[i], r_st[i]], axis=0).astype(BF16),
                 jnp.concatenate([b_t[i], b_t[i], k_t[i], k_t[i]], axis=0).astype(BF16)) for i in rng]
    m_ab = [jnp.where(bd_strict, z[:two, :two], 0.0) for z in x]
    m_akv = [_dot(jnp.where(bd_strict, x[i][:two, two:], 0.0).astype(BF16), v_st_b[i]) for i in rng]
    n_cat = [jnp.concatenate([jnp.where(bd_incl, z[two:, :two], 0.0),
                              jnp.where(bd_incl, z[two:, two:], 0.0)], axis=1).astype(BF16) for z in x]
    t_inv = [eye2 + m for m in m_ab]
    pw = m_ab
    steps = 1
    while steps * 2 < seq:
        pw_b = [p.astype(BF16) for p in pw]
        pw = [_dot(p, p) for p in pw_b]
        t_inv = [_dot(t_inv[i].astype(BF16), (eye2 + pw[i]).astype(BF16)) for i in rng]
        steps *= 2
    ty = [_dot(t_inv[i].astype(BF16), jnp.concatenate([a_st[i], m_akv[i]], axis=1).astype(BF16))
          for i in rng]
    ah_b = [z[:, :PAIR_LANES].astype(BF16) for z in ty]
    uh_b = [z[:, PAIR_LANES:].astype(BF16) for z in ty]
    p_t = [_dot_tn(bh_st[i], ah_b[i]) + jnp.where(eye_s, gam_last[i], 0.0) for i in rng]
    q_t = [_dot_tn(jnp.concatenate([bh_st[i], kh_st[i]], axis=0),
                   jnp.concatenate([uh_b[i], v_st_b[i]], axis=0)) for i in rng]
    ro = [_dot(n_cat[i], jnp.concatenate(
        [jnp.concatenate([ah_b[i], uh_b[i]], axis=1),
         jnp.concatenate([jnp.zeros_like(v_st_b[i]), v_st_b[i]], axis=1)], axis=0)) for i in rng]
    outs, states = [], []
    for i in rng:
        s_hi, s_lo = _split2(s_list[i])
        rh_hi, rh_lo = _split2(r_st[i] + ro[i][:, :PAIR_LANES])
        o_st = _dot(rh_hi, s_hi) + _dot(rh_hi, s_lo) + _dot(rh_lo, s_hi) + ro[i][:, PAIR_LANES:]
        outs.append(o_st[:seq] + o_st[seq:])
        p_hi, p_lo = _split2(p_t[i])
        states.append(_dot(p_hi, s_hi) + _dot(p_hi, s_lo) + _dot(p_lo, s_hi) + q_t[i])
    return outs, states


def _wkv_kernel(r_ref, lw_ref, k_ref, v_ref, kk_ref, kka_ref, s0_ref, o_ref, s1_ref, s_scr):
    j = pl.program_id(1)
    nb, seq = r_ref.shape[0], r_ref.shape[1]
    two = 2 * seq

    @pl.when(j == 0)
    def _():
        s_scr[...] = s0_ref[...]

    ri = lax.broadcasted_iota(jnp.int32, (seq, seq), 0)
    ci = lax.broadcasted_iota(jnp.int32, (seq, seq), 1)
    tri = (ri >= ci).astype(BF16)
    r2 = lax.broadcasted_iota(jnp.int32, (two, two), 0)
    c2 = lax.broadcasted_iota(jnp.int32, (two, two), 1)
    same = (r2 // seq) == (c2 // seq)
    bd_strict = same & ((r2 % seq) > (c2 % seq))
    bd_incl = same & ((r2 % seq) >= (c2 % seq))
    eye2 = (r2 == c2).astype(F32)
    rs = lax.broadcasted_iota(jnp.int32, (PAIR_LANES, PAIR_LANES), 0)
    cs = lax.broadcasted_iota(jnp.int32, (PAIR_LANES, PAIR_LANES), 1)
    eye_s = rs == cs
    lane_lo = lax.broadcasted_iota(jnp.int32, (seq, PAIR_LANES), 1) < HEAD_DIM

    ids = [(b, pair) for b in range(nb) for pair in range(HEAD_PAIRS)]
    lanes = lambda pair: slice(pair * PAIR_LANES, (pair + 1) * PAIR_LANES)
    units = [tuple(ref[b, :, lanes(pair)] for ref in (r_ref, lw_ref, k_ref, v_ref, kk_ref, kka_ref))
             for (b, pair) in ids]
    s_list = [s_scr[b, pair] for (b, pair) in ids]
    outs, states = _wkv_units(units, s_list, tri, bd_strict, bd_incl, eye2, eye_s, lane_lo)
    for (b, pair), o, s_new in zip(ids, outs, states):
        o_ref[b, :, lanes(pair)] = o
        s_scr[b, pair] = s_new

    @pl.when(j == pl.num_programs(1) - 1)
    def _():
        s1_ref[...] = s_scr[...]


WKV_BATCH_PER_STEP = 2


def _wkv(r, lw, k, v, kk, kka, s0_bd):
    bsz, t_len, _ = r.shape
    nc = t_len // WKV_CHUNK
    nb = WKV_BATCH_PER_STEP
    tile = pl.BlockSpec((nb, WKV_CHUNK, RWKV_WIDTH), lambda b, j: (b, j, 0))
    st = pl.BlockSpec((nb, HEAD_PAIRS, PAIR_LANES, PAIR_LANES), lambda b, j: (b, 0, 0, 0))
    return pl.pallas_call(
        _wkv_kernel,
        grid=(bsz // nb, nc),
        in_specs=[tile] * 6 + [st],
        out_specs=[tile, st],
        out_shape=[
            jax.ShapeDtypeStruct((bsz, t_len, RWKV_WIDTH), F32),
            jax.ShapeDtypeStruct((bsz, HEAD_PAIRS, PAIR_LANES, PAIR_LANES), F32),
        ],
        scratch_shapes=[pltpu.VMEM((nb, HEAD_PAIRS, PAIR_LANES, PAIR_LANES), F32)],
        compiler_params=_params(("parallel", "arbitrary")),
        name="wkv",
    )(r, lw, k, v, kk, kka, s0_bd)


def _s5_kernel(u_ref, x0re_ref, x0im_ref, wst_ref, wy_ref, tmat_ref, dre_ref, dim_ref,
               y_ref, x1re_ref, x1im_ref, zre_scr, zim_scr, *, bsz, nc):
    half = 2 * S5_STATE
    u2 = jnp.concatenate([u_ref[0], u_ref[1]], axis=1).astype(BF16)
    z = _dot(u2, wst_ref[0])
    zre_scr[...] = z[:, :half]
    zim_scr[...] = z[:, half:]
    dre = dre_ref[0]
    dim = dim_ref[0]

    def body(j, carry):
        new = []
        for b in range(bsz):
            xre, xim = carry[2 * b], carry[2 * b + 1]
            row = pl.ds(b * nc + j, 1)
            zre = zre_scr[row, :]
            zim = zim_scr[row, :]
            zre_scr[row, :] = xre
            zim_scr[row, :] = xim
            new.append(dre * xre - dim * xim + zre)
            new.append(dre * xim + dim * xre + zim)
        return tuple(new)

    init = []
    for b in range(bsz):
        init.append(x0re_ref[0, b:b + 1, :])
        init.append(x0im_ref[0, b:b + 1, :])
    fin = lax.fori_loop(0, nc, body, tuple(init))
    for b in range(bsz):
        x1re_ref[0, b:b + 1, :] = fin[2 * b]
        x1im_ref[0, b:b + 1, :] = fin[2 * b + 1]
    xin = jnp.concatenate([zre_scr[...], zim_scr[...]], axis=1).astype(BF16)
    y = _dot(u2, tmat_ref[0]) + _dot(xin, wy_ref[0])
    width = S5_CHUNK * S5_CH
    y_ref[0] = y[:, :width]
    y_ref[1] = y[:, width:]


def _s5(uf, x0re, x0im, mats, bsz, nc):
    rows = bsz * nc
    gp = S5_GROUPS // 2
    width = S5_CHUNK * S5_CH
    half = 2 * S5_STATE
    st = pl.BlockSpec((1, bsz, half), lambda g: (g, 0, 0))
    dec = pl.BlockSpec((1, 1, half), lambda g: (g, 0, 0))
    return pl.pallas_call(
        functools.partial(_s5_kernel, bsz=bsz, nc=nc),
        grid=(gp,),
        in_specs=[
            pl.BlockSpec((2, rows, width), lambda g: (g, 0, 0)),
            st, st,
            pl.BlockSpec((1, 2 * width, 2 * half), lambda g: (g, 0, 0)),
            pl.BlockSpec((1, 2 * half, 2 * width), lambda g: (g, 0, 0)),
            pl.BlockSpec((1, 2 * width, 2 * width), lambda g: (g, 0, 0)),
            dec, dec,
        ],
        out_specs=[pl.BlockSpec((2, rows, width), lambda g: (g, 0, 0)), st, st],
        out_shape=[
            jax.ShapeDtypeStruct((S5_GROUPS, rows, width), F32),
            jax.ShapeDtypeStruct((gp, bsz, half), F32),
            jax.ShapeDtypeStruct((gp, bsz, half), F32),
        ],
        scratch_shapes=[pltpu.VMEM((rows, half), F32), pltpu.VMEM((rows, half), F32)],
        compiler_params=_params(("parallel",)),
        name="s5",
    )(uf, x0re, x0im, mats["wst"], mats["wy"], mats["tmat"], mats["dre"], mats["dim"])


def _s5_matrices(a_re, a_im, log_dt, b_re, b_im, c_re, c_im, d):
    L = S5_CHUNK
    dt = jnp.exp(log_dt)[:, None]
    n = jnp.arange(L + 1, dtype=F32)[:, None, None]
    mag = jnp.exp(a_re * dt * n)
    ang = a_im * dt * n
    pw_re, pw_im = mag * jnp.cos(ang), mag * jnp.sin(ang)
    num_re, num_im = pw_re[1] - 1.0, pw_im[1]
    den = a_re * a_re + a_im * a_im
    f_re = (num_re * a_re + num_im * a_im) / den
    f_im = (num_im * a_re - num_re * a_im) / den
    bb_re = f_re[..., None] * b_re - f_im[..., None] * b_im
    bb_im = f_re[..., None] * b_im + f_im[..., None] * b_re
    pb_re = pw_re[..., None] * bb_re - pw_im[..., None] * bb_im
    pb_im = pw_re[..., None] * bb_im + pw_im[..., None] * bb_re
    kern = (jnp.einsum('gcp,ngpd->ngcd', c_re, pb_re[:L]) - jnp.einsum('gcp,ngpd->ngcd', c_im, pb_im[:L]))
    s_idx = jnp.arange(L)[:, None]
    t_idx = jnp.arange(L)[None, :]
    lag = jnp.clip(t_idx - s_idx, 0, L - 1)
    tk = kern[lag]
    tk = jnp.where((t_idx >= s_idx)[:, :, None, None, None], tk, 0.0)
    tmat = tk.transpose(2, 0, 4, 1, 3)
    eye_st = jnp.eye(L, dtype=F32)[:, None, :, None] * jnp.eye(S5_CH, dtype=F32)[None, :, None, :]
    tmat = tmat + d.reshape(S5_GROUPS, 1, 1, 1, S5_CH) * eye_st[None]
    tmat = tmat.reshape(S5_GROUPS, L * S5_CH, L * S5_CH)
    inj_re = pb_re[:L][::-1].transpose(1, 0, 3, 2).reshape(S5_GROUPS, L * S5_CH, S5_STATE)
    inj_im = pb_im[:L][::-1].transpose(1, 0, 3, 2).reshape(S5_GROUPS, L * S5_CH, S5_STATE)
    cp_re = (c_re[None] * pw_re[1:, :, None, :] - c_im[None] * pw_im[1:, :, None, :])
    cp_im = (c_re[None] * pw_im[1:, :, None, :] + c_im[None] * pw_re[1:, :, None, :])
    rd_re = cp_re.transpose(1, 3, 0, 2).reshape(S5_GROUPS, S5_STATE, L * S5_CH)
    rd_im = -cp_im.transpose(1, 3, 0, 2).reshape(S5_GROUPS, S5_STATE, L * S5_CH)

    gp = S5_GROUPS // 2
    width = L * S5_CH
    z = jnp.zeros((gp, width, S5_STATE), F32)
    pair = lambda m: m.reshape(gp, 2, *m.shape[1:])
    ir, ii = pair(inj_re), pair(inj_im)
    wst = jnp.concatenate([
        jnp.concatenate([ir[:, 0], z, ii[:, 0], z], axis=2),
        jnp.concatenate([z, ir[:, 1], z, ii[:, 1]], axis=2)], axis=1)
    zr = jnp.zeros((gp, S5_STATE, width), F32)
    rr, ri = pair(rd_re), pair(rd_im)
    wy = jnp.concatenate([
        jnp.concatenate([rr[:, 0], zr], axis=2),
        jnp.concatenate([zr, rr[:, 1]], axis=2),
        jnp.concatenate([ri[:, 0], zr], axis=2),
        jnp.concatenate([zr, ri[:, 1]], axis=2)], axis=1)
    tp = pair(tmat)
    zt = jnp.zeros((gp, width, width), F32)
    tm2 = jnp.concatenate([
        jnp.concatenate([tp[:, 0], zt], axis=2),
        jnp.concatenate([zt, tp[:, 1]], axis=2)], axis=1)
    dre = pw_re[L].reshape(gp, 1, 2 * S5_STATE)
    dim = pw_im[L].reshape(gp, 1, 2 * S5_STATE)
    return {"wst": wst.astype(BF16), "wy": wy.astype(BF16), "tmat": tm2.astype(BF16),
            "dre": dre, "dim": dim}


def _mix_out_kernel(o_ref, bonus_ref, g_ref, y_ref, x_ref, gnw_ref, gnb_ref, wglu_ref, bglu_ref,
                    wout_ref, ones_ref, x1_ref):
    ones_bd = ones_ref[...]
    o = o_ref[...]
    mu = _head_sum(o, ones_bd) * (1.0 / HEAD_DIM)
    d = o - mu
    var = _head_sum(d * d, ones_bd) * (1.0 / HEAD_DIM)
    o = d * lax.rsqrt(var + GN_EPS) * gnw_ref[...] + gnb_ref[...]
    o = (o + bonus_ref[...]) * g_ref[...]
    y = _gelu(y_ref[...])
    y = y * _sigmoid(_dot(y.astype(BF16), wglu_ref[...]) + bglu_ref[...])
    mixed = jnp.concatenate([o, y], axis=1).astype(BF16)
    x1_ref[...] = x_ref[...] + _dot(mixed, wout_ref[...])


def _mix_out(o, bonus, g, y, x2d, prm, tm):
    n = x2d.shape[0]
    tile = pl.BlockSpec((tm, RWKV_WIDTH), lambda i: (i, 0))
    row = pl.BlockSpec((1, RWKV_WIDTH), lambda i: (0, 0))
    sq = pl.BlockSpec((RWKV_WIDTH, RWKV_WIDTH), lambda i: (0, 0))
    return pl.pallas_call(
        _mix_out_kernel,
        grid=(n // tm,),
        in_specs=[tile, tile, tile, tile, pl.BlockSpec((tm, D_MODEL), lambda i: (i, 0)),
                  row, row, sq, row, pl.BlockSpec((D_MODEL, D_MODEL), lambda i: (0, 0)), sq],
        out_specs=pl.BlockSpec((tm, D_MODEL), lambda i: (i, 0)),
        out_shape=jax.ShapeDtypeStruct((n, D_MODEL), F32),
        compiler_params=_params(("parallel",)),
        name="mix_out",
    )(o, bonus, g, y, x2d, prm["gn_w"], prm["gn_b"], prm["w_glu"], prm["b_glu"], prm["w_out"],
      prm["ones_bd"])


def _top16(x):
    out = []
    for _ in range(PEER_TOPK):
        m = jnp.max(x, axis=0, keepdims=True)
        out.append(m)
        x = jnp.where(x == m, -jnp.inf, x)
    return out


def _peer_route_kernel(x_ref, g_ref, wqt_ref, k1_ref, k2_ref,
                       h2t_ref, s1_ref, e1_ref, s2_ref, e2_ref, thr_ref):
    h2 = _rmsnorm(x_ref[...], g_ref[...])
    h2_b = h2.astype(BF16)
    h2t_ref[...] = jnp.transpose(h2).astype(BF16)
    qt = _dot_nt(wqt_ref[...], h2_b)
    for h in range(PEER_HEADS):
        q1 = qt[h * PEER_QDIM:h * PEER_QDIM + PEER_HALF].astype(BF16)
        q2 = qt[h * PEER_QDIM + PEER_HALF:(h + 1) * PEER_QDIM].astype(BF16)
        s1 = _dot(k1_ref[h], q1)
        s2 = _dot(k2_ref[h], q2)
        a1 = _top16(s1)
        a2 = _top16(s2)
        a2_lo = jnp.concatenate(a2[:8], axis=0)
        cand = [a1[0] + a2_lo, a1[0] + jnp.concatenate(a2[8:], axis=0)]
        cand += [a1[i] + a2_lo for i in range(1, 8)]
        cand.append(jnp.concatenate(a1[8:], axis=0) + a2[0])
        cand = jnp.concatenate(cand, axis=0)
        top = a1[0] + a2[0]
        zsum = jnp.zeros_like(top)
        thr = top
        for _ in range(PEER_TOPK):
            thr = jnp.max(cand, axis=0, keepdims=True)
            zsum = zsum + jnp.exp(thr - top)
            cand = jnp.where(cand == thr, -jnp.inf, cand)
        s1_ref[h] = s1
        s2_ref[h] = s2
        e1_ref[h] = jnp.exp(s1 - a1[0])
        e2_ref[h] = jnp.exp(s2 - a2[0]) / zsum
        thr_ref[h] = thr


def _peer_route(x1, ln2_g, wqt_bf, k1_bf, k2_bf, tt):
    n = x1.shape[0]
    keys = pl.BlockSpec((PEER_HEADS, N_KEYS, PEER_HALF), lambda i: (0, 0, 0))
    tok = pl.BlockSpec((PEER_HEADS, N_KEYS, tt), lambda i: (0, 0, i))
    return pl.pallas_call(
        _peer_route_kernel,
        grid=(n // tt,),
        in_specs=[
            pl.BlockSpec((tt, D_MODEL), lambda i: (i, 0)),
            pl.BlockSpec((1, D_MODEL), lambda i: (0, 0)),
            pl.BlockSpec((PEER_HEADS * PEER_QDIM, D_MODEL), lambda i: (0, 0)),
            keys, keys,
        ],
        out_specs=[pl.BlockSpec((D_MODEL, tt), lambda i: (0, i)), tok, tok, tok, tok,
                   pl.BlockSpec((PEER_HEADS, 1, tt), lambda i: (0, 0, i))],
        out_shape=[jax.ShapeDtypeStruct((D_MODEL, n), BF16)]
        + [jax.ShapeDtypeStruct((PEER_HEADS, N_KEYS, n), F32)] * 4
        + [jax.ShapeDtypeStruct((PEER_HEADS, 1, n), F32)],
        compiler_params=_params(("parallel",)),
        name="peer_route",
    )(x1, ln2_g, wqt_bf, k1_bf, k2_bf)


PEER_CHUNK = 512
PEER_STEP_KEYS = 2 * PEER_CHUNK // N_KEYS
PEER_PIECE_ROWS = 16
PEER_PIECE_LANES = 128


def _peer_build(a_scr, w_scr, s1_ref, e1_ref, s2_ref, e2_ref, thr_ref, key, row, toks):
    for lane0 in range(toks.start, toks.stop, PEER_PIECE_LANES):
        lanes = slice(lane0, lane0 + PEER_PIECE_LANES)
        for sub in range(0, N_KEYS, PEER_PIECE_ROWS):
            second = slice(sub, sub + PEER_PIECE_ROWS)
            gate = None
            for h in range(PEER_HEADS):
                score = s2_ref[h, second, lanes] + s1_ref[h, key:key + 1, lanes]
                sel = jnp.where(score >= thr_ref[h, :, lanes], e2_ref[h, second, lanes], 0.0)
                term = sel * e1_ref[h, key:key + 1, lanes]
                gate = term if gate is None else gate + term
            rows = slice(row * N_KEYS + sub, row * N_KEYS + sub + PEER_PIECE_ROWS)
            w_scr[rows, lanes] = (_gelu(a_scr[rows, lanes]) * gate).astype(BF16)


def _peer_dense_kernel(h2t_ref, u_first_ref, u_odd_ref, u_next_ref, vt_prev_ref, vt_even_ref, vt_last_ref,
                       s1_ref, e1_ref, s2_ref, e2_ref, thr_ref, x1_ref, g_ref,
                       y_ref, acc_scr, a0_scr, a1_scr, w0_scr, w1_scr):
    k = pl.program_id(1)
    routing = (s1_ref, e1_ref, s2_ref, e2_ref, thr_ref)

    @pl.when(k == 0)
    def _():
        acc_scr[...] = jnp.zeros_like(acc_scr)
        w1_scr[...] = jnp.zeros_like(w1_scr)
        a0_scr[...] = _dot(u_first_ref[...], h2t_ref[...])

    tt = h2t_ref.shape[1]
    nslab = 2 if tt % 512 == 0 else 1
    slabs = [slice(s * (tt // nslab), (s + 1) * (tt // nslab)) for s in range(nslab)]
    keys_per_chunk = PEER_CHUNK // N_KEYS

    def accumulate(vt_ref, w_scr, toks):
        acc_scr[:, toks] += _dot(vt_ref[...], w_scr[:, toks])

    def activations(u_ref, a_scr, toks):
        a_scr[:, toks] = _dot(u_ref[...], h2t_ref[:, toks])

    matmuls = ([functools.partial(accumulate, vt_prev_ref, w1_scr, t) for t in slabs]
               + [functools.partial(activations, u_odd_ref, a1_scr, t) for t in slabs]
               + [functools.partial(accumulate, vt_even_ref, w0_scr, t) for t in slabs]
               + [functools.partial(activations, u_next_ref, a0_scr, t) for t in slabs])
    builds = ([functools.partial(_peer_build, a0_scr, w0_scr, *routing, i, i, t)
               for t in slabs for i in range(keys_per_chunk)]
              + [functools.partial(_peer_build, a1_scr, w1_scr, *routing, keys_per_chunk + i, i, t)
                 for t in slabs for i in range(keys_per_chunk)])
    per = len(builds) // len(matmuls)
    for m, piece in enumerate(matmuls):
        piece()
        for build in builds[m * per:(m + 1) * per]:
            build()

    @pl.when(k == pl.num_programs(1) - 1)
    def _():
        acc = acc_scr[...] + _dot(vt_last_ref[...], w1_scr[...])
        x2 = x1_ref[...] + jnp.transpose(acc)
        y_ref[...] = _rmsnorm(x2, g_ref[...])


def _peer_dense(h2t, u_bf, vt_bf, s1, e1, s2, e2, thr, x1, lnf_g, tt):
    n = x1.shape[0]
    nchunks = N_EXPERTS // PEER_CHUNK
    nsteps = nchunks // 2
    by_step = lambda z: z.reshape(PEER_HEADS, nsteps, PEER_STEP_KEYS, n)
    u_spec = lambda f: pl.BlockSpec((PEER_CHUNK, D_MODEL), lambda i, k: (f(k), 0))
    vt_spec = lambda f: pl.BlockSpec((D_MODEL, PEER_CHUNK), lambda i, k: (0, f(k)))
    tok_all = pl.BlockSpec((PEER_HEADS, N_KEYS, tt), lambda i, k: (0, 0, i))
    tok_step = pl.BlockSpec((PEER_HEADS, None, PEER_STEP_KEYS, tt), lambda i, k: (0, k, 0, i))
    return pl.pallas_call(
        _peer_dense_kernel,
        grid=(n // tt, nsteps),
        in_specs=[
            pl.BlockSpec((D_MODEL, tt), lambda i, k: (0, i)),
            u_spec(lambda k: 0),
            u_spec(lambda k: 2 * k + 1),
            u_spec(lambda k: jnp.minimum(2 * k + 2, nchunks - 1)),
            vt_spec(lambda k: jnp.maximum(2 * k - 1, 0)),
            vt_spec(lambda k: 2 * k),
            vt_spec(lambda k: nchunks - 1),
            tok_step, tok_step, tok_all, tok_all,
            pl.BlockSpec((PEER_HEADS, 1, tt), lambda i, k: (0, 0, i)),
            pl.BlockSpec((tt, D_MODEL), lambda i, k: (i, 0)),
            pl.BlockSpec((1, D_MODEL), lambda i, k: (0, 0)),
        ],
        out_specs=pl.BlockSpec((tt, D_MODEL), lambda i, k: (i, 0)),
        out_shape=jax.ShapeDtypeStruct((n, D_MODEL), F32),
        scratch_shapes=[pltpu.VMEM((D_MODEL, tt), F32),
                        pltpu.VMEM((PEER_CHUNK, tt), F32), pltpu.VMEM((PEER_CHUNK, tt), F32),
                        pltpu.VMEM((PEER_CHUNK, tt), BF16), pltpu.VMEM((PEER_CHUNK, tt), BF16)],
        compiler_params=_params(("parallel", "arbitrary")),
        name="peer_dense",
    )(h2t, u_bf, u_bf, u_bf, vt_bf, vt_bf, vt_bf, by_step(s1), by_step(e1), s2, e2, thr, x1, lnf_g)


def _pick_tile(n, target):
    t = min(n, target)
    while n % t:
        t //= 2
    return t


def _layer(x, wkv0, shift0, s5re0, s5im0, prm, s5m, lnf_g):
    bsz, t_len, _ = x.shape
    n = bsz * t_len
    x2d = x.reshape(n, D_MODEL)
    tm = _pick_tile(t_len, 512)
    prw, u = _in_proj(x2d, prm["ln1_g"], prm["w_in"], tm)

    tiles_per_seq = t_len // tm
    prw3 = prw.reshape(bsz, t_len, RWKV_PROJ)
    last_rows = prw3[:, tm - 1::tm, :]
    bound = jnp.concatenate([shift0, last_rows[:, :tiles_per_seq - 1, :]], axis=1)
    bound = bound.reshape(bsz * tiles_per_seq, 1, RWKV_PROJ)
    r, lw, k, v, kk, kka, g, bonus = _rwkv_prep(prw, bound, prm, tm)

    t_pad = -(-t_len // WKV_CHUNK) * WKV_CHUNK
    seq3 = lambda z: jnp.pad(z.reshape(bsz, t_len, RWKV_WIDTH), ((0, 0), (0, t_pad - t_len), (0, 0)))
    st = wkv0.reshape(bsz, HEAD_PAIRS, 2, HEAD_DIM, HEAD_DIM).transpose(0, 1, 2, 4, 3)
    eye2 = jnp.eye(2, dtype=F32)
    s0_bd = jnp.einsum('bphkv,hg->bphkgv', st, eye2).reshape(bsz, HEAD_PAIRS, PAIR_LANES, PAIR_LANES)
    o, s1_bd = _wkv(seq3(r), seq3(lw), seq3(k), seq3(v), seq3(kk), seq3(kka), s0_bd)
    o = o[:, :t_len].reshape(n, RWKV_WIDTH)
    s1 = s1_bd.reshape(bsz, HEAD_PAIRS, 2, HEAD_DIM, 2, HEAD_DIM)
    s1 = jnp.stack([s1[:, :, 0, :, 0, :], s1[:, :, 1, :, 1, :]], axis=2)
    wkv1 = s1.transpose(0, 1, 2, 4, 3).reshape(bsz, RWKV_HEADS, HEAD_DIM, HEAD_DIM)

    nc = t_len // S5_CHUNK
    uf = u.reshape(bsz, nc, S5_CHUNK, S5_GROUPS, S5_CH).transpose(3, 0, 1, 2, 4)
    uf = uf.reshape(S5_GROUPS, bsz * nc, S5_CHUNK * S5_CH)
    gp = S5_GROUPS // 2
    pk = lambda z: z.reshape(bsz, gp, 2 * S5_STATE).transpose(1, 0, 2)
    yf, x1re, x1im = _s5(uf, pk(s5re0), pk(s5im0), s5m, bsz, nc)
    y = yf.reshape(S5_GROUPS, bsz, nc, S5_CHUNK, S5_CH).transpose(1, 2, 3, 0, 4).reshape(n, S5_WIDTH)
    unpk = lambda z: z.transpose(1, 0, 2).reshape(bsz, S5_GROUPS, S5_STATE)

    x1 = _mix_out(o, bonus, g, y, x2d, prm, tm)

    tt = _pick_tile(n, 512)
    h2t, ps1, pe1, ps2, pe2, thr = _peer_route(x1, prm["ln2_g"], prm["w_qt"], prm["keys1"], prm["keys2"], tt)
    yout = _peer_dense(h2t, prm["peer_u"], prm["peer_vt"], ps1, pe1, ps2, pe2, thr, x1, lnf_g, tt)
    shift1 = prw3[:, -1:, :]
    return yout.reshape(bsz, t_len, D_MODEL), wkv1, shift1, unpk(x1re), unpk(x1im)


def kernel(x_prompt, x_sample, state_wkv, state_shift, state_s5_re, state_s5_im, ln1_g, w_in, rwkv_mu, rwkv_w0, rwkv_w2, rwkv_a0, rwkv_a2, rwkv_g2, rwkv_k_k, rwkv_k_a, rwkv_r_k, rwkv_gn_w, rwkv_gn_b, s5_a_re, s5_a_im, s5_log_dt, s5_b_re, s5_b_im, s5_c_re, s5_c_im, s5_d, s5_w_glu, s5_b_glu, w_out, ln2_g, peer_w_q, peer_keys1, peer_keys2, peer_u, peer_v, lnf_g):
    depth = w_in.shape[0]
    assert depth == 1
    l = 0
    row = lambda z: z.reshape(1, -1).astype(F32)
    lora = jnp.zeros((LORA_WIDTH, 3 * RWKV_WIDTH), F32)
    lora = lora.at[:64, :RWKV_WIDTH].set(rwkv_w2[l])
    lora = lora.at[64:128, RWKV_WIDTH:2 * RWKV_WIDTH].set(rwkv_a2[l])
    lora = lora.at[128:, 2 * RWKV_WIDTH:].set(rwkv_g2[l])
    lane = jnp.arange(RWKV_WIDTH)
    ones_bd = (lane[:, None] // HEAD_DIM == lane[None, :] // HEAD_DIM).astype(BF16)
    prm = {
        "ln1_g": row(ln1_g[l]), "w_in": w_in[l].astype(BF16), "mu": row(rwkv_mu[l]),
        "w0": row(rwkv_w0[l]), "a0": row(rwkv_a0[l]), "lora": lora.astype(BF16),
        "k_k": row(rwkv_k_k[l]), "k_a": row(rwkv_k_a[l]), "r_k": row(rwkv_r_k[l]),
        "gn_w": row(rwkv_gn_w[l]), "gn_b": row(rwkv_gn_b[l]), "ones_bd": ones_bd,
        "w_glu": s5_w_glu[l].astype(BF16), "b_glu": row(s5_b_glu[l]), "w_out": w_out[l].astype(BF16),
        "ln2_g": row(ln2_g[l]), "w_qt": peer_w_q[l].T.astype(BF16),
        "keys1": peer_keys1[l].astype(BF16), "keys2": peer_keys2[l].astype(BF16),
        "peer_u": peer_u[l].astype(BF16), "peer_vt": peer_v[l].T.astype(BF16),
    }
    s5m = _s5_matrices(s5_a_re[l], s5_a_im[l], s5_log_dt[l], s5_b_re[l], s5_b_im[l], s5_c_re[l],
                       s5_c_im[l], s5_d[l])
    lnf = row(lnf_g)
    bp = x_prompt.shape[0]
    zeros = lambda *s: jnp.zeros(s, F32)
    yp, a1, a2, a3, a4 = _layer(
        x_prompt, zeros(bp, RWKV_HEADS, HEAD_DIM, HEAD_DIM), zeros(bp, 1, RWKV_PROJ),
        zeros(bp, S5_GROUPS, S5_STATE), zeros(bp, S5_GROUPS, S5_STATE), prm, s5m, lnf)
    ys, b1, b2, b3, b4 = _layer(x_sample, state_wkv[l], state_shift[l], state_s5_re[l], state_s5_im[l],
                                prm, s5m, lnf)
    st = lambda z: z[None]
    return (yp, ys, st(a1), st(a2), st(a3), st(a4), st(b1), st(b2), st(b3), st(b4))
```

```python
import functools
import math

import jax
import jax.numpy as jnp
from jax import lax
from jax.experimental import pallas as pl
from jax.experimental.pallas import tpu as pltpu

F32 = jnp.float32
BF16 = jnp.bfloat16

D_MODEL = 1024
RWKV_WIDTH = 512
HEAD_DIM = 64
RWKV_HEADS = 8
HEAD_PAIRS = RWKV_HEADS // 2
PAIR_LANES = 2 * HEAD_DIM
LORA_WIDTH = 256
RWKV_PROJ = 3 * RWKV_WIDTH + LORA_WIDTH
S5_WIDTH = 512
S5_CH = 16
S5_GROUPS = 32
S5_STATE = 64
S5_CHUNK = 16
IN_PROJ = RWKV_PROJ + S5_WIDTH
PEER_HEADS = 8
N_KEYS = 128
N_EXPERTS = N_KEYS * N_KEYS
PEER_TOPK = 16
PEER_QDIM = 256
PEER_HALF = 128
NORM_EPS = 1e-6
GN_EPS = HEAD_DIM * 1e-5
WKV_CHUNK = 64

VMEM_LIMIT_BYTES = 48 * 1024 * 1024


def _params(semantics, flags=None):
    return pltpu.CompilerParams(dimension_semantics=semantics, vmem_limit_bytes=VMEM_LIMIT_BYTES,
                                flags=flags)


def _dot(a, b):
    return jnp.dot(a, b, preferred_element_type=F32)


def _dot_nt(a, b):
    return lax.dot_general(a, b, (((1,), (1,)), ((), ())), preferred_element_type=F32)


def _dot_tn(a, b):
    return lax.dot_general(a, b, (((0,), (0,)), ((), ())), preferred_element_type=F32)


def _split2(x):
    hi = x.astype(BF16)
    lo = (x - hi.astype(F32)).astype(BF16)
    return hi, lo


def _split3(x):
    hi = x.astype(BF16)
    r1 = x - hi.astype(F32)
    mid = r1.astype(BF16)
    lo = (r1 - mid.astype(F32)).astype(BF16)
    return hi, mid, lo


def _head_sum(x, ones_bd):
    hi, lo = _split2(x)
    return _dot(hi, ones_bd) + _dot(lo, ones_bd)


def _rmsnorm(x, g):
    return x * lax.rsqrt(jnp.mean(x * x, axis=-1, keepdims=True) + NORM_EPS) * g


def _gelu(x):
    c = math.sqrt(2.0 / math.pi)
    half = 0.5 * x
    return half * jnp.tanh(x * (c + (c * 0.044715) * (x * x))) + half


def _sigmoid(x):
    return 1.0 / (1.0 + jnp.exp(-x))


def _in_proj_kernel(x_ref, g_ref, w_ref, prw_ref, u_ref):
    h = _rmsnorm(x_ref[...], g_ref[...]).astype(BF16)
    p = _dot(h, w_ref[...])
    prw_ref[...] = p[:, :RWKV_PROJ]
    u_ref[...] = p[:, RWKV_PROJ:]


def _in_proj(x2d, ln1_g, w_in_bf, tm):
    n = x2d.shape[0]
    return pl.pallas_call(
        _in_proj_kernel,
        grid=(n // tm,),
        in_specs=[
            pl.BlockSpec((tm, D_MODEL), lambda i: (i, 0)),
            pl.BlockSpec((1, D_MODEL), lambda i: (0, 0)),
            pl.BlockSpec((D_MODEL, IN_PROJ), lambda i: (0, 0)),
        ],
        out_specs=[
            pl.BlockSpec((tm, RWKV_PROJ), lambda i: (i, 0)),
            pl.BlockSpec((tm, S5_WIDTH), lambda i: (i, 0)),
        ],
        out_shape=[
            jax.ShapeDtypeStruct((n, RWKV_PROJ), F32),
            jax.ShapeDtypeStruct((n, S5_WIDTH), F32),
        ],
        compiler_params=_params(("parallel",)),
        name="in_proj",
    )(x2d, ln1_g, w_in_bf)


def _rwkv_prep_kernel(p_ref, bound_ref, mu_ref, w0_ref, a0_ref, lora_ref, kk_ref_, ka_ref, rk_ref,
                      ones_ref, r_out, lw_out, k_out, v_out, kk_out, kka_out, g_out, bonus_out):
    p = p_ref[...]
    tm = p.shape[0]
    row = lax.broadcasted_iota(jnp.int32, p.shape, 0)
    p_prev = jnp.where(row == 0, bound_ref[0], pltpu.roll(p, 1, axis=0))
    m = p + (p_prev - p) * mu_ref[...]
    r = m[:, :RWKV_WIDTH]
    k = m[:, RWKV_WIDTH:2 * RWKV_WIDTH]
    v = m[:, 2 * RWKV_WIDTH:3 * RWKV_WIDTH]
    z = m[:, 3 * RWKV_WIDTH:]
    lane = lax.broadcasted_iota(jnp.int32, (tm, LORA_WIDTH), 1)
    feat = jnp.where(lane < 64, jnp.tanh(z), jnp.where(lane < 128, z, _sigmoid(z)))
    lo = _dot(feat.astype(BF16), lora_ref[...])
    wl = -(w0_ref[...] + lo[:, :RWKV_WIDTH])
    softplus = jnp.maximum(wl, 0.0) + jnp.log(1.0 + jnp.exp(-jnp.abs(wl)))
    lw_out[...] = -jnp.exp(-softplus - 0.5)
    a = _sigmoid(a0_ref[...] + lo[:, RWKV_WIDTH:2 * RWKV_WIDTH])
    g_out[...] = lo[:, 2 * RWKV_WIDTH:]
    ones_bd = ones_ref[...]
    kk = k * kk_ref_[...]
    nrm = jnp.sqrt(_head_sum(kk * kk, ones_bd))
    kk = kk / jnp.maximum(nrm, 1e-12)
    k2 = k * (1.0 + (a - 1.0) * ka_ref[...])
    r_out[...] = r
    k_out[...] = k2
    v_out[...] = v
    kk_out[...] = kk
    kka_out[...] = kk * a
    bonus_out[...] = _head_sum(r * k2 * rk_ref[...], ones_bd) * v


def _rwkv_prep(prw, bound, prm, tm):
    n = prw.shape[0]
    row = lambda width: pl.BlockSpec((1, width), lambda i: (0, 0))
    tile = pl.BlockSpec((tm, RWKV_WIDTH), lambda i: (i, 0))
    return pl.pallas_call(
        _rwkv_prep_kernel,
        grid=(n // tm,),
        in_specs=[
            pl.BlockSpec((tm, RWKV_PROJ), lambda i: (i, 0)),
            pl.BlockSpec((1, 1, RWKV_PROJ), lambda i: (i, 0, 0)),
            row(RWKV_PROJ), row(RWKV_WIDTH), row(RWKV_WIDTH),
            pl.BlockSpec((LORA_WIDTH, 3 * RWKV_WIDTH), lambda i: (0, 0)),
            row(RWKV_WIDTH), row(RWKV_WIDTH), row(RWKV_WIDTH),
            pl.BlockSpec((RWKV_WIDTH, RWKV_WIDTH), lambda i: (0, 0)),
        ],
        out_specs=[tile] * 8,
        out_shape=[jax.ShapeDtypeStruct((n, RWKV_WIDTH), F32)] * 8,
        compiler_params=_params(("parallel",)),
        name="rwkv_prep",
    )(prw, bound, prm["mu"], prm["w0"], prm["a0"], prm["lora"], prm["k_k"], prm["k_a"], prm["r_k"],
      prm["ones_bd"])


def _wkv_units(units, s_list, tri, bd_strict, bd_incl, eye2, eye_s, lane_lo):
    seq = units[0][0].shape[0]
    two = 2 * seq
    n = len(units)
    rng = range(n)

    def stack(z):
        return jnp.concatenate([jnp.where(lane_lo, z, 0.0), jnp.where(lane_lo, 0.0, z)], axis=0)

    cl = []
    for (r, lw, k, v, kk, kka) in units:
        hi, mid, lo = _split3(lw)
        cl.append(_dot(tri, hi) + _dot(tri, mid) + _dot(tri, lo))
    gam = [jnp.exp(c) for c in cl]
    gam_inv = [jnp.exp(-c) for c in cl]
    gam_prev = [jnp.exp(cl[i] - units[i][1]) for i in rng]
    gam_last = [g[seq - 1:seq, :] for g in gam]
    a_t = [-units[i][4] * gam_prev[i] for i in rng]
    r_t = [units[i][0] * gam[i] for i in rng]
    b_t = [units[i][5] * gam_inv[i] for i in rng]
    k_t = [units[i][2] * gam_inv[i] for i in rng]
    a_st = [stack(z) for z in a_t]
    r_st = [stack(z) for z in r_t]
    v_st_b = [stack(u[3]).astype(BF16) for u in units]
    bh_st = [stack(b_t[i] * gam_last[i]).astype(BF16) for i in rng]
    kh_st = [stack(k_t[i] * gam_last[i]).astype(BF16) for i in rng]
    x = [_dot_nt(jnp.concatenate([a_st[i], r_st[i]], axis=0).astype(BF16),
                 jnp.concatenate([b_t[i], b_t[i], k_t[i], k_t[i]], axis=0).astype(BF16)) for i in rng]
    m_ab = [jnp.where(bd_strict, z[:two, :two], 0.0) for z in x]
    m_akv = [_dot(jnp.where(bd_strict, x[i][:two, two:], 0.0).astype(BF16), v_st_b[i]) for i in rng]
    n_cat = [jnp.concatenate([jnp.where(bd_incl, z[two:, :two], 0.0),
                              jnp.where(bd_incl, z[two:, two:], 0.0)], axis=1).astype(BF16) for z in x]
    t_inv = [eye2 + m for m in m_ab]
    pw = m_ab
    steps = 1
    while steps * 2 < seq:
        pw_b = [p.astype(BF16) for p in pw]
        pw = [_dot(p, p) for p in pw_b]
        t_inv = [_dot(t_inv[i].astype(BF16), (eye2 + pw[i]).astype(BF16)) for i in rng]
        steps *= 2
    ty = [_dot(t_inv[i].astype(BF16), jnp.concatenate([a_st[i], m_akv[i]], axis=1).astype(BF16))
          for i in rng]
    ah_b = [z[:, :PAIR_LANES].astype(BF16) for z in ty]
    uh_b = [z[:, PAIR_LANES:].astype(BF16) for z in ty]
    p_t = [_dot_tn(bh_st[i], ah_b[i]) + jnp.where(eye_s, gam_last[i], 0.0) for i in rng]
    q_t = [_dot_tn(jnp.concatenate([bh_st[i], kh_st[i]], axis=0),
                   jnp.concatenate([uh_b[i], v_st_b[i]], axis=0)) for i in rng]
    ro = [_dot(n_cat[i], jnp.concatenate(
        [jnp.concatenate([ah_b[i], uh_b[i]], axis=1),
         jnp.concatenate([jnp.zeros_like(v_st_b[i]), v_st_b[i]], axis=1)], axis=0)) for i in rng]
    outs, states = [], []
    for i in rng:
        s_hi, s_lo = _split2(s_list[i])
        rh_hi, rh_lo = _split2(r_st[i] + ro[i][:, :PAIR_LANES])
        o_st = _dot(rh_hi, s_hi) + _dot(rh_hi, s_lo) + _dot(rh_lo, s_hi) + ro[i][:, PAIR_LANES:]
        outs.append(o_st[:seq] + o_st[seq:])
        p_hi, p_lo = _split2(p_t[i])
        states.append(_dot(p_hi, s_hi) + _dot(p_hi, s_lo) + _dot(p_lo, s_hi) + q_t[i])
    return outs, states


def _wkv_kernel(r_ref, lw_ref, k_ref, v_ref, kk_ref, kka_ref, s0_ref, o_ref, s1_ref, s_scr):
    j = pl.program_id(1)
    nb, seq = r_ref.shape[0], r_ref.shape[1]
    two = 2 * seq

    @pl.when(j == 0)
    def _():
        s_scr[...] = s0_ref[...]

    ri = lax.broadcasted_iota(jnp.int32, (seq, seq), 0)
    ci = lax.broadcasted_iota(jnp.int32, (seq, seq), 1)
    tri = (ri >= ci).astype(BF16)
    r2 = lax.broadcasted_iota(jnp.int32, (two, two), 0)
    c2 = lax.broadcasted_iota(jnp.int32, (two, two), 1)
    same = (r2 // seq) == (c2 // seq)
    bd_strict = same & ((r2 % seq) > (c2 % seq))
    bd_incl = same & ((r2 % seq) >= (c2 % seq))
    eye2 = (r2 == c2).astype(F32)
    rs = lax.broadcasted_iota(jnp.int32, (PAIR_LANES, PAIR_LANES), 0)
    cs = lax.broadcasted_iota(jnp.int32, (PAIR_LANES, PAIR_LANES), 1)
    eye_s = rs == cs
    lane_lo = lax.broadcasted_iota(jnp.int32, (seq, PAIR_LANES), 1) < HEAD_DIM

    ids = [(b, pair) for b in range(nb) for pair in range(HEAD_PAIRS)]
    lanes = lambda pair: slice(pair * PAIR_LANES, (pair + 1) * PAIR_LANES)
    units = [tuple(ref[b, :, lanes(pair)] for ref in (r_ref, lw_ref, k_ref, v_ref, kk_ref, kka_ref))
             for (b, pair) in ids]
    s_list = [s_scr[b, pair] for (b, pair) in ids]
    outs, states = _wkv_units(units, s_list, tri, bd_strict, bd_incl, eye2, eye_s, lane_lo)
    for (b, pair), o, s_new in zip(ids, outs, states):
        o_ref[b, :, lanes(pair)] = o
        s_scr[b, pair] = s_new

    @pl.when(j == pl.num_programs(1) - 1)
    def _():
        s1_ref[...] = s_scr[...]


WKV_BATCH_PER_STEP = 2


def _wkv(r, lw, k, v, kk, kka, s0_bd):
    bsz, t_len, _ = r.shape
    nc = t_len // WKV_CHUNK
    nb = WKV_BATCH_PER_STEP
    tile = pl.BlockSpec((nb, WKV_CHUNK, RWKV_WIDTH), lambda b, j: (b, j, 0))
    st = pl.BlockSpec((nb, HEAD_PAIRS, PAIR_LANES, PAIR_LANES), lambda b, j: (b, 0, 0, 0))
    return pl.pallas_call(
        _wkv_kernel,
        grid=(bsz // nb, nc),
        in_specs=[tile] * 6 + [st],
        out_specs=[tile, st],
        out_shape=[
            jax.ShapeDtypeStruct((bsz, t_len, RWKV_WIDTH), F32),
            jax.ShapeDtypeStruct((bsz, HEAD_PAIRS, PAIR_LANES, PAIR_LANES), F32),
        ],
        scratch_shapes=[pltpu.VMEM((nb, HEAD_PAIRS, PAIR_LANES, PAIR_LANES), F32)],
        compiler_params=_params(("parallel", "arbitrary")),
        name="wkv",
    )(r, lw, k, v, kk, kka, s0_bd)


S5_LANE_BLOCKS = S5_WIDTH // 128
S5_BLOCK_GROUPS = 128 // S5_CH
S5_BLOCK_STATE = S5_BLOCK_GROUPS * S5_STATE
S5_ROWS_PER_STEP = 256


def _s5_kernel(u_ref, x0_ref, wst_ref, wy_ref, tmat_ref, dec_ref, y_ref, x1_ref, z_scr, x_scr, *, nc):
    r = pl.program_id(1)
    rows = u_ref.shape[0]
    half = S5_BLOCK_STATE
    u2 = jnp.concatenate([u_ref[:, s, :] for s in range(S5_CHUNK)], axis=1).astype(BF16)
    z_scr[...] = _dot(u2, wst_ref[...])
    dre = dec_ref[:, :half]
    dim = dec_ref[:, half:]
    row0 = r * rows

    def body(i, carry):
        xre, xim = carry
        row = row0 + i
        b = row // nc
        start = (row % nc) == 0
        x0 = x0_ref[pl.ds(b, 1), :]
        xre = jnp.where(start, x0[:, :half], xre)
        xim = jnp.where(start, x0[:, half:], xim)
        zrow = z_scr[pl.ds(i, 1), :]
        z_scr[pl.ds(i, 1), :] = jnp.concatenate([xre, xim], axis=1)
        nre = dre * xre - dim * xim + zrow[:, :half]
        nim = dre * xim + dim * xre + zrow[:, half:]
        x1_ref[pl.ds(b, 1), :] = jnp.concatenate([nre, nim], axis=1)
        return nre, nim

    fin = lax.fori_loop(0, rows, body, (x_scr[0:1, :], x_scr[1:2, :]))
    x_scr[0:1, :] = fin[0]
    x_scr[1:2, :] = fin[1]
    y = _dot(u2, tmat_ref[...]) + _dot(z_scr[...].astype(BF16), wy_ref[...])
    for t in range(S5_CHUNK):
        y_ref[:, t, :] = y[:, t * 128:(t + 1) * 128]


def _s5(u3, x0, mats, nc):
    rows_all = u3.shape[0]
    bsz = rows_all // nc
    rows = min(S5_ROWS_PER_STEP, rows_all)
    width = S5_CHUNK * 128
    state = 2 * S5_BLOCK_STATE
    const = lambda shape: pl.BlockSpec((None,) + shape, lambda j, r: (j, 0, 0))
    tile = pl.BlockSpec((rows, S5_CHUNK, 128), lambda j, r: (r, 0, j))
    return pl.pallas_call(
        functools.partial(_s5_kernel, nc=nc),
        grid=(S5_LANE_BLOCKS, rows_all // rows),
        in_specs=[tile, const((bsz, state)), const((width, state)), const((state, width)),
                  const((width, width)), const((1, state))],
        out_specs=[tile, const((bsz, state))],
        out_shape=[jax.ShapeDtypeStruct(u3.shape, F32),
                   jax.ShapeDtypeStruct((S5_LANE_BLOCKS, bsz, state), F32)],
        scratch_shapes=[pltpu.VMEM((rows, state), F32), pltpu.VMEM((8, S5_BLOCK_STATE), F32)],
        compiler_params=_params(("parallel", "arbitrary")),
        name="s5",
    )(u3, x0, mats["wst"], mats["wy"], mats["tmat"], mats["dec"])


def _s5_matrices(a_re, a_im, log_dt, b_re, b_im, c_re, c_im, d):
    L = S5_CHUNK
    dt = jnp.exp(log_dt)[:, None]
    n = jnp.arange(L + 1, dtype=F32)[:, None, None]
    mag = jnp.exp(a_re * dt * n)
    ang = a_im * dt * n
    pw_re, pw_im = mag * jnp.cos(ang), mag * jnp.sin(ang)
    num_re, num_im = pw_re[1] - 1.0, pw_im[1]
    den = a_re * a_re + a_im * a_im
    f_re = (num_re * a_re + num_im * a_im) / den
    f_im = (num_im * a_re - num_re * a_im) / den
    bb_re = f_re[..., None] * b_re - f_im[..., None] * b_im
    bb_im = f_re[..., None] * b_im + f_im[..., None] * b_re
    pb_re = pw_re[..., None] * bb_re - pw_im[..., None] * bb_im
    pb_im = pw_re[..., None] * bb_im + pw_im[..., None] * bb_re
    kern = (jnp.einsum('gcp,ngpd->ngcd', c_re, pb_re[:L]) - jnp.einsum('gcp,ngpd->ngcd', c_im, pb_im[:L]))
    s_idx = jnp.arange(L)[:, None]
    t_idx = jnp.arange(L)[None, :]
    tk = kern[jnp.clip(t_idx - s_idx, 0, L - 1)]
    tk = jnp.where((t_idx >= s_idx)[:, :, None, None, None], tk, 0.0)
    skip = (jnp.eye(L, dtype=F32)[:, :, None, None, None] * d.reshape(1, 1, S5_GROUPS, S5_CH, 1)
            * jnp.eye(S5_CH, dtype=F32)[None, None, None])
    tk = tk + skip
    nb, bg = S5_LANE_BLOCKS, S5_BLOCK_GROUPS

    def block_diag(compact, row_inner, outer, inner):
        src = jnp.arange(outer * inner)
        dst = jnp.arange(outer * bg * inner)
        pick = ((src[:, None] // inner == dst[None, :] // (bg * inner))
                & (src[:, None] % inner == dst[None, :] % inner)).astype(BF16)
        wide = jnp.einsum('jrk,kn->jrn', compact.astype(BF16), pick, preferred_element_type=F32)
        row_group = (jnp.arange(compact.shape[1]) // row_inner) % bg
        keep = row_group[:, None] == (dst[None, :] // inner) % bg
        return jnp.where(keep, wide, 0.0).astype(BF16)

    tk = tk.reshape(L, L, nb, bg, S5_CH, S5_CH).transpose(2, 0, 3, 5, 1, 4)
    tmat = block_diag(tk.reshape(nb, L * 128, L * S5_CH), S5_CH, L, S5_CH)
    inj = jnp.stack([pb_re[:L][::-1], pb_im[:L][::-1]], axis=0)
    inj = inj.reshape(2, L, nb, bg, S5_STATE, S5_CH).transpose(2, 1, 3, 5, 0, 4)
    wst = block_diag(inj.reshape(nb, L * 128, 2 * S5_STATE), S5_CH, 2, S5_STATE)
    cp_re = c_re[None] * pw_re[1:, :, None, :] - c_im[None] * pw_im[1:, :, None, :]
    cp_im = c_re[None] * pw_im[1:, :, None, :] + c_im[None] * pw_re[1:, :, None, :]
    rd = jnp.stack([cp_re, -cp_im], axis=0).reshape(2, L, nb, bg, S5_CH, S5_STATE)
    rd = rd.transpose(2, 0, 3, 5, 1, 4)
    wy = block_diag(rd.reshape(nb, 2 * bg * S5_STATE, L * S5_CH), S5_STATE, L, S5_CH)
    dec = jnp.concatenate([pw_re[L].reshape(nb, 1, bg * S5_STATE), pw_im[L].reshape(nb, 1, bg * S5_STATE)],
                          axis=2)
    return {"wst": wst.astype(BF16), "wy": wy.astype(BF16), "tmat": tmat.astype(BF16), "dec": dec}


def _mix_out_kernel(o_ref, bonus_ref, g_ref, y_ref, x_ref, gnw_ref, gnb_ref, wglu_ref, bglu_ref,
                    wout_ref, ones_ref, x1_ref):
    ones_bd = ones_ref[...]
    o = o_ref[...]
    mu = _head_sum(o, ones_bd) * (1.0 / HEAD_DIM)
    d = o - mu
    var = _head_sum(d * d, ones_bd) * (1.0 / HEAD_DIM)
    o = d * lax.rsqrt(var + GN_EPS) * gnw_ref[...] + gnb_ref[...]
    o = (o + bonus_ref[...]) * g_ref[...]
    y = _gelu(y_ref[...])
    y = y * _sigmoid(_dot(y.astype(BF16), wglu_ref[...]) + bglu_ref[...])
    mixed = jnp.concatenate([o, y], axis=1).astype(BF16)
    x1_ref[...] = x_ref[...] + _dot(mixed, wout_ref[...])


def _mix_out(o, bonus, g, y, x2d, prm, tm):
    n = x2d.shape[0]
    tile = pl.BlockSpec((tm, RWKV_WIDTH), lambda i: (i, 0))
    row = pl.BlockSpec((1, RWKV_WIDTH), lambda i: (0, 0))
    sq = pl.BlockSpec((RWKV_WIDTH, RWKV_WIDTH), lambda i: (0, 0))
    return pl.pallas_call(
        _mix_out_kernel,
        grid=(n // tm,),
        in_specs=[tile, tile, tile, tile, pl.BlockSpec((tm, D_MODEL), lambda i: (i, 0)),
                  row, row, sq, row, pl.BlockSpec((D_MODEL, D_MODEL), lambda i: (0, 0)), sq],
        out_specs=pl.BlockSpec((tm, D_MODEL), lambda i: (i, 0)),
        out_shape=jax.ShapeDtypeStruct((n, D_MODEL), F32),
        compiler_params=_params(("parallel",)),
        name="mix_out",
    )(o, bonus, g, y, x2d, prm["gn_w"], prm["gn_b"], prm["w_glu"], prm["b_glu"], prm["w_out"],
      prm["ones_bd"])


def _sort16_pairs():
    def merge(lo, hi, r):
        step = 2 * r
        if step < hi - lo:
            yield from merge(lo, hi, step)
            yield from merge(lo + r, hi, step)
            yield from ((i, i + r) for i in range(lo + r, hi - r, step))
        else:
            yield (lo, lo + r)

    def sort(lo, hi):
        if hi > lo:
            mid = lo + (hi - lo) // 2
            yield from sort(lo, mid)
            yield from sort(mid + 1, hi)
            yield from merge(lo, hi, 1)

    return list(sort(0, PEER_TOPK - 1))


def _top16(x):
    sub = x.shape[0] // PEER_TOPK
    v = [x[k * sub:(k + 1) * sub, :] for k in range(PEER_TOPK)]
    for i, j in _sort16_pairs():
        v[i], v[j] = jnp.maximum(v[i], v[j]), jnp.minimum(v[i], v[j])
    vals = []
    for q in range(PEER_TOPK):
        m = jnp.max(v[0], axis=0, keepdims=True)
        vals.append(m)
        if q + 1 < PEER_TOPK:
            hit = v[0] == m
            for k in range(PEER_TOPK - 1 - q):
                v[k] = jnp.where(hit, v[k + 1], v[k])
    return vals


def _peer_route_kernel(x_ref, g_ref, wqt_ref, k1_ref, k2_ref,
                       h2t_ref, e1_ref, cnt_ref, rank2_ref, e2_ref):
    h2 = _rmsnorm(x_ref[...], g_ref[...])
    h2_b = h2.astype(BF16)
    h2t_ref[...] = jnp.transpose(h2).astype(BF16)
    qt = _dot_nt(wqt_ref[...], h2_b)
    for h in range(PEER_HEADS):
        q1 = qt[h * PEER_QDIM:h * PEER_QDIM + PEER_HALF].astype(BF16)
        q2 = qt[h * PEER_QDIM + PEER_HALF:(h + 1) * PEER_QDIM].astype(BF16)
        s1 = _dot(k1_ref[h], q1)
        s2 = _dot(k2_ref[h], q2)
        a1 = _top16(s1)
        a2 = _top16(s2)
        a1_mat = jnp.concatenate(a1, axis=0)
        a2_lo = jnp.concatenate(a2[:8], axis=0)
        cand = [a1[0] + a2_lo, a1[0] + jnp.concatenate(a2[8:], axis=0)]
        cand += [a1[i] + a2_lo for i in range(1, 8)]
        cand.append(a1_mat[8:] + a2[0])
        cand = jnp.concatenate(cand, axis=0)
        top = a1[0] + a2[0]
        zsum = jnp.zeros_like(top)
        thr = top
        for _ in range(PEER_TOPK):
            thr = jnp.max(cand, axis=0, keepdims=True)
            zsum = zsum + jnp.exp(thr - top)
            cand = jnp.where(cand == thr, -jnp.inf, cand)
        cnt = jnp.zeros_like(s1)
        for q in range(PEER_TOPK):
            reach = (a1_mat + a2[q]) >= thr
            b_q = jnp.min(jnp.where(reach, a1_mat, jnp.inf), axis=0, keepdims=True)
            cnt = jnp.where(s1 >= b_q, float(q + 1), cnt)
        rank2 = jnp.full(s2.shape, float(PEER_TOPK), F32)
        for q in reversed(range(PEER_TOPK)):
            rank2 = jnp.where(s2 >= a2[q], float(q), rank2)
        e1_ref[h] = jnp.exp(s1 - a1[0])
        cnt_ref[h] = cnt
        rank2_ref[h] = rank2.astype(BF16)
        e2_ref[h] = (jnp.exp(s2 - a2[0]) / zsum).astype(BF16)


def _peer_route(x1, ln2_g, wqt_bf, k1_bf, k2_bf, tt):
    n = x1.shape[0]
    keys = pl.BlockSpec((PEER_HEADS, N_KEYS, PEER_HALF), lambda i: (0, 0, 0))
    tok = pl.BlockSpec((PEER_HEADS, N_KEYS, tt), lambda i: (0, 0, i))
    per_key = lambda dt: jax.ShapeDtypeStruct((PEER_HEADS, N_KEYS, n), dt)
    return pl.pallas_call(
        _peer_route_kernel,
        grid=(n // tt,),
        in_specs=[
            pl.BlockSpec((tt, D_MODEL), lambda i: (i, 0)),
            pl.BlockSpec((1, D_MODEL), lambda i: (0, 0)),
            pl.BlockSpec((PEER_HEADS * PEER_QDIM, D_MODEL), lambda i: (0, 0)),
            keys, keys,
        ],
        out_specs=[pl.BlockSpec((D_MODEL, tt), lambda i: (0, i)), tok, tok, tok, tok],
        out_shape=[jax.ShapeDtypeStruct((D_MODEL, n), BF16), per_key(F32), per_key(F32), per_key(BF16),
                   per_key(BF16)],
        compiler_params=_params(("parallel",)),
        name="peer_route",
    )(x1, ln2_g, wqt_bf, k1_bf, k2_bf)


PEER_CHUNK = 1024
PEER_CHUNK_KEYS = PEER_CHUNK // N_KEYS
PEER_PIECE_ROWS = 16
PEER_PIECE_LANES = 128


def _peer_dense_kernel(h2t_ref, u_ref, vt_ref, e1_ref, cnt_ref, rank2_in_ref, e2_in_ref, x1_ref, g_ref,
                       y_ref, acc_scr, a_scr, w_scr, rank2_ref, e2_ref):
    c = pl.program_id(1)
    tt = h2t_ref.shape[1]

    @pl.when(c == 0)
    def _():
        acc_scr[...] = jnp.zeros_like(acc_scr)
        rank2_ref[...] = rank2_in_ref[...]
        e2_ref[...] = e2_in_ref[...]

    a_scr[...] = _dot(u_ref[...], h2t_ref[...])

    def build(key, carry):
        base = pl.multiple_of(key * N_KEYS, N_KEYS)
        cnt_rows = [cnt_ref[h, pl.ds(key, 1), :] for h in range(PEER_HEADS)]
        e1_rows = [e1_ref[h, pl.ds(key, 1), :] for h in range(PEER_HEADS)]
        for lane0 in range(0, tt, PEER_PIECE_LANES):
            lanes = slice(lane0, lane0 + PEER_PIECE_LANES)
            tile = (PEER_PIECE_ROWS, PEER_PIECE_LANES)
            cnt_b = [jnp.broadcast_to(cnt_rows[h][:, lanes], tile).astype(BF16) for h in range(PEER_HEADS)]
            e1_b = [jnp.broadcast_to(e1_rows[h][:, lanes], tile).astype(BF16) for h in range(PEER_HEADS)]
            for sub in range(0, N_KEYS, PEER_PIECE_ROWS):
                second = slice(sub, sub + PEER_PIECE_ROWS)
                gate = None
                for h in range(PEER_HEADS):
                    sel = jnp.where(rank2_ref[h, second, lanes] < cnt_b[h], e2_ref[h, second, lanes],
                                    jnp.zeros(tile, BF16))
                    term = sel * e1_b[h]
                    gate = term if gate is None else gate + term
                rows = pl.ds(base + sub, PEER_PIECE_ROWS)
                w_scr[rows, lanes] = _gelu(a_scr[rows, lanes]).astype(BF16) * gate
        return carry

    lax.fori_loop(0, PEER_CHUNK_KEYS, build, 0)
    acc_scr[...] += _dot(vt_ref[...], w_scr[...])

    @pl.when(c == pl.num_programs(1) - 1)
    def _():
        x2 = x1_ref[...] + jnp.transpose(acc_scr[...])
        y_ref[...] = _rmsnorm(x2, g_ref[...])


def _peer_dense(h2t, u_bf, vt_bf, e1, cnt, rank2, e2, x1, lnf_g, tt):
    n = x1.shape[0]
    nchunks = N_EXPERTS // PEER_CHUNK
    by_chunk = lambda z: z.reshape(PEER_HEADS, nchunks, PEER_CHUNK_KEYS, n)
    tok_all = pl.BlockSpec((PEER_HEADS, N_KEYS, tt), lambda i, c: (0, 0, i))
    tok_chunk = pl.BlockSpec((PEER_HEADS, None, PEER_CHUNK_KEYS, tt), lambda i, c: (0, c, 0, i))
    return pl.pallas_call(
        _peer_dense_kernel,
        grid=(n // tt, nchunks),
        in_specs=[
            pl.BlockSpec((D_MODEL, tt), lambda i, c: (0, i)),
            pl.BlockSpec((PEER_CHUNK, D_MODEL), lambda i, c: (c, 0)),
            pl.BlockSpec((None, D_MODEL, PEER_CHUNK), lambda i, c: (c, 0, 0)),
            tok_chunk, tok_chunk, tok_all, tok_all,
            pl.BlockSpec((tt, D_MODEL), lambda i, c: (i, 0)),
            pl.BlockSpec((1, D_MODEL), lambda i, c: (0, 0)),
        ],
        out_specs=pl.BlockSpec((tt, D_MODEL), lambda i, c: (i, 0)),
        out_shape=jax.ShapeDtypeStruct((n, D_MODEL), F32),
        scratch_shapes=[pltpu.VMEM((D_MODEL, tt), F32), pltpu.VMEM((PEER_CHUNK, tt), F32),
                        pltpu.VMEM((PEER_CHUNK, tt), BF16),
                        pltpu.VMEM((PEER_HEADS, N_KEYS, tt), BF16), pltpu.VMEM((PEER_HEADS, N_KEYS, tt), BF16)],
        compiler_params=_params(("parallel", "arbitrary")),
        name="peer_dense",
    )(h2t, u_bf, vt_bf, by_chunk(e1), by_chunk(cnt), rank2, e2, x1, lnf_g)


def _pick_tile(n, target):
    t = min(n, target)
    while n % t:
        t //= 2
    return t


def _layer(x, wkv0, shift0, s5re0, s5im0, prm, s5m, lnf_g):
    bsz, t_len, _ = x.shape
    n = bsz * t_len
    x2d = x.reshape(n, D_MODEL)
    tm = _pick_tile(t_len, 512)
    prw, u = _in_proj(x2d, prm["ln1_g"], prm["w_in"], tm)

    tiles_per_seq = t_len // tm
    prw3 = prw.reshape(bsz, t_len, RWKV_PROJ)
    last_rows = prw3[:, tm - 1::tm, :]
    bound = jnp.concatenate([shift0, last_rows[:, :tiles_per_seq - 1, :]], axis=1)
    bound = bound.reshape(bsz * tiles_per_seq, 1, RWKV_PROJ)
    r, lw, k, v, kk, kka, g, bonus = _rwkv_prep(prw, bound, prm, tm)

    t_pad = -(-t_len // WKV_CHUNK) * WKV_CHUNK
    seq3 = lambda z: jnp.pad(z.reshape(bsz, t_len, RWKV_WIDTH), ((0, 0), (0, t_pad - t_len), (0, 0)))
    st = wkv0.reshape(bsz, HEAD_PAIRS, 2, HEAD_DIM, HEAD_DIM).transpose(0, 1, 2, 4, 3)
    eye2 = jnp.eye(2, dtype=F32)
    s0_bd = jnp.einsum('bphkv,hg->bphkgv', st, eye2).reshape(bsz, HEAD_PAIRS, PAIR_LANES, PAIR_LANES)
    o, s1_bd = _wkv(seq3(r), seq3(lw), seq3(k), seq3(v), seq3(kk), seq3(kka), s0_bd)
    o = o[:, :t_len].reshape(n, RWKV_WIDTH)
    s1 = s1_bd.reshape(bsz, HEAD_PAIRS, 2, HEAD_DIM, 2, HEAD_DIM)
    s1 = jnp.stack([s1[:, :, 0, :, 0, :], s1[:, :, 1, :, 1, :]], axis=2)
    wkv1 = s1.transpose(0, 1, 2, 4, 3).reshape(bsz, RWKV_HEADS, HEAD_DIM, HEAD_DIM)

    nc = t_len // S5_CHUNK
    blk = lambda z: z.reshape(bsz, S5_LANE_BLOCKS, S5_BLOCK_STATE).transpose(1, 0, 2)
    x0 = jnp.concatenate([blk(s5re0), blk(s5im0)], axis=2)
    y3, x1s = _s5(u.reshape(bsz * nc, S5_CHUNK, S5_WIDTH), x0, s5m, nc)
    y = y3.reshape(n, S5_WIDTH)
    unblk = lambda z: z.transpose(1, 0, 2).reshape(bsz, S5_GROUPS, S5_STATE)
    s5re1, s5im1 = unblk(x1s[:, :, :S5_BLOCK_STATE]), unblk(x1s[:, :, S5_BLOCK_STATE:])

    x1 = _mix_out(o, bonus, g, y, x2d, prm, tm)

    tt = _pick_tile(n, 512)
    h2t, e1, cnt, rank2, e2 = _peer_route(x1, prm["ln2_g"], prm["w_qt"], prm["keys1"], prm["keys2"], tt)
    yout = _peer_dense(h2t, prm["peer_u"], prm["peer_vt"], e1, cnt, rank2, e2, x1, lnf_g, tt)
    shift1 = prw3[:, -1:, :]
    return yout.reshape(bsz, t_len, D_MODEL), wkv1, shift1, s5re1, s5im1


def kernel(x_prompt, x_sample, state_wkv, state_shift, state_s5_re, state_s5_im, ln1_g, w_in, rwkv_mu, rwkv_w0, rwkv_w2, rwkv_a0, rwkv_a2, rwkv_g2, rwkv_k_k, rwkv_k_a, rwkv_r_k, rwkv_gn_w, rwkv_gn_b, s5_a_re, s5_a_im, s5_log_dt, s5_b_re, s5_b_im, s5_c_re, s5_c_im, s5_d, s5_w_glu, s5_b_glu, w_out, ln2_g, peer_w_q, peer_keys1, peer_keys2, peer_u, peer_v, lnf_g):
    depth = w_in.shape[0]
    assert depth == 1
    l = 0
    row = lambda z: z.reshape(1, -1).astype(F32)
    lora = jnp.zeros((LORA_WIDTH, 3 * RWKV_WIDTH), F32)
    lora = lora.at[:64, :RWKV_WIDTH].set(rwkv_w2[l])
    lora = lora.at[64:128, RWKV_WIDTH:2 * RWKV_WIDTH].set(rwkv_a2[l])
    lora = lora.at[128:, 2 * RWKV_WIDTH:].set(rwkv_g2[l])
    lane = jnp.arange(RWKV_WIDTH)
    ones_bd = (lane[:, None] // HEAD_DIM == lane[None, :] // HEAD_DIM).astype(BF16)
    prm = {
        "ln1_g": row(ln1_g[l]), "w_in": w_in[l].astype(BF16), "mu": row(rwkv_mu[l]),
        "w0": row(rwkv_w0[l]), "a0": row(rwkv_a0[l]), "lora": lora.astype(BF16),
        "k_k": row(rwkv_k_k[l]), "k_a": row(rwkv_k_a[l]), "r_k": row(rwkv_r_k[l]),
        "gn_w": row(rwkv_gn_w[l]), "gn_b": row(rwkv_gn_b[l]), "ones_bd": ones_bd,
        "w_glu": s5_w_glu[l].astype(BF16), "b_glu": row(s5_b_glu[l]), "w_out": w_out[l].astype(BF16),
        "ln2_g": row(ln2_g[l]), "w_qt": peer_w_q[l].T.astype(BF16),
        "keys1": peer_keys1[l].astype(BF16), "keys2": peer_keys2[l].astype(BF16),
        "peer_u": peer_u[l].astype(BF16),
        "peer_vt": peer_v[l].astype(BF16).reshape(N_EXPERTS // PEER_CHUNK, PEER_CHUNK, D_MODEL)
        .transpose(0, 2, 1),
    }
    s5m = _s5_matrices(s5_a_re[l], s5_a_im[l], s5_log_dt[l], s5_b_re[l], s5_b_im[l], s5_c_re[l],
                       s5_c_im[l], s5_d[l])
    lnf = row(lnf_g)
    bp = x_prompt.shape[0]
    zeros = lambda *s: jnp.zeros(s, F32)
    yp, a1, a2, a3, a4 = _layer(
        x_prompt, zeros(bp, RWKV_HEADS, HEAD_DIM, HEAD_DIM), zeros(bp, 1, RWKV_PROJ),
        zeros(bp, S5_GROUPS, S5_STATE), zeros(bp, S5_GROUPS, S5_STATE), prm, s5m, lnf)
    ys, b1, b2, b3, b4 = _layer(x_sample, state_wkv[l], state_shift[l], state_s5_re[l], state_s5_im[l],
                                prm, s5m, lnf)
    st = lambda z: z[None]
    return (yp, ys, st(a1), st(a2), st(a3), st(a4), st(b1), st(b2), st(b3), st(b4))
```

```python
import functools
import math

import jax
import jax.numpy as jnp
from jax import lax
from jax.experimental import pallas as pl
from jax.experimental.pallas import tpu as pltpu

F32 = jnp.float32
BF16 = jnp.bfloat16

D_MODEL = 1024
RWKV_WIDTH = 512
HEAD_DIM = 64
RWKV_HEADS = 8
HEAD_PAIRS = RWKV_HEADS // 2
PAIR_LANES = 2 * HEAD_DIM
LORA_WIDTH = 256
RWKV_PROJ = 3 * RWKV_WIDTH + LORA_WIDTH
S5_WIDTH = 512
S5_CH = 16
S5_GROUPS = 32
S5_STATE = 64
S5_CHUNK = 16
IN_PROJ = RWKV_PROJ + S5_WIDTH
PEER_HEADS = 8
N_KEYS = 128
N_EXPERTS = N_KEYS * N_KEYS
PEER_TOPK = 16
PEER_QDIM = 256
PEER_HALF = 128
NORM_EPS = 1e-6
GN_EPS = HEAD_DIM * 1e-5
WKV_CHUNK = 64

VMEM_LIMIT_BYTES = 48 * 1024 * 1024


def _params(semantics, flags=None):
    return pltpu.CompilerParams(dimension_semantics=semantics, vmem_limit_bytes=VMEM_LIMIT_BYTES,
                                flags=flags)


def _dot(a, b):
    return jnp.dot(a, b, preferred_element_type=F32)


def _dot_nt(a, b):
    return lax.dot_general(a, b, (((1,), (1,)), ((), ())), preferred_element_type=F32)


def _dot_tn(a, b):
    return lax.dot_general(a, b, (((0,), (0,)), ((), ())), preferred_element_type=F32)


def _split2(x):
    hi = x.astype(BF16)
    lo = (x - hi.astype(F32)).astype(BF16)
    return hi, lo


def _split3(x):
    hi = x.astype(BF16)
    r1 = x - hi.astype(F32)
    mid = r1.astype(BF16)
    lo = (r1 - mid.astype(F32)).astype(BF16)
    return hi, mid, lo


def _head_sum(x, ones_bd):
    hi, lo = _split2(x)
    return _dot(hi, ones_bd) + _dot(lo, ones_bd)


def _rmsnorm(x, g):
    return x * lax.rsqrt(jnp.mean(x * x, axis=-1, keepdims=True) + NORM_EPS) * g


def _gelu(x):
    c = math.sqrt(2.0 / math.pi)
    half = 0.5 * x
    return half * jnp.tanh(x * (c + (c * 0.044715) * (x * x))) + half


def _sigmoid(x):
    return 1.0 / (1.0 + jnp.exp(-x))


def _in_proj_kernel(x_ref, g_ref, w_ref, prw_ref, u_ref):
    h = _rmsnorm(x_ref[...], g_ref[...]).astype(BF16)
    p = _dot(h, w_ref[...])
    prw_ref[...] = p[:, :RWKV_PROJ]
    u_ref[...] = p[:, RWKV_PROJ:]


def _in_proj(x2d, ln1_g, w_in_bf, tm):
    n = x2d.shape[0]
    return pl.pallas_call(
        _in_proj_kernel,
        grid=(n // tm,),
        in_specs=[
            pl.BlockSpec((tm, D_MODEL), lambda i: (i, 0)),
            pl.BlockSpec((1, D_MODEL), lambda i: (0, 0)),
            pl.BlockSpec((D_MODEL, IN_PROJ), lambda i: (0, 0)),
        ],
        out_specs=[
            pl.BlockSpec((tm, RWKV_PROJ), lambda i: (i, 0)),
            pl.BlockSpec((tm, S5_WIDTH), lambda i: (i, 0)),
        ],
        out_shape=[
            jax.ShapeDtypeStruct((n, RWKV_PROJ), F32),
            jax.ShapeDtypeStruct((n, S5_WIDTH), F32),
        ],
        compiler_params=_params(("parallel",)),
        name="in_proj",
    )(x2d, ln1_g, w_in_bf)


def _rwkv_prep_kernel(p_ref, bound_ref, mu_ref, w0_ref, a0_ref, lora_ref, kk_ref_, ka_ref, rk_ref,
                      ones_ref, r_out, lw_out, k_out, v_out, kk_out, kka_out, g_out, bonus_out):
    p = p_ref[...]
    tm = p.shape[0]
    row = lax.broadcasted_iota(jnp.int32, p.shape, 0)
    p_prev = jnp.where(row == 0, bound_ref[0], pltpu.roll(p, 1, axis=0))
    m = p + (p_prev - p) * mu_ref[...]
    r = m[:, :RWKV_WIDTH]
    k = m[:, RWKV_WIDTH:2 * RWKV_WIDTH]
    v = m[:, 2 * RWKV_WIDTH:3 * RWKV_WIDTH]
    z = m[:, 3 * RWKV_WIDTH:]
    lane = lax.broadcasted_iota(jnp.int32, (tm, LORA_WIDTH), 1)
    feat = jnp.where(lane < 64, jnp.tanh(z), jnp.where(lane < 128, z, _sigmoid(z)))
    lo = _dot(feat.astype(BF16), lora_ref[...])
    wl = -(w0_ref[...] + lo[:, :RWKV_WIDTH])
    softplus = jnp.maximum(wl, 0.0) + jnp.log(1.0 + jnp.exp(-jnp.abs(wl)))
    lw_out[...] = -jnp.exp(-softplus - 0.5)
    a = _sigmoid(a0_ref[...] + lo[:, RWKV_WIDTH:2 * RWKV_WIDTH])
    g_out[...] = lo[:, 2 * RWKV_WIDTH:]
    ones_bd = ones_ref[...]
    kk = k * kk_ref_[...]
    nrm = jnp.sqrt(_head_sum(kk * kk, ones_bd))
    kk = kk / jnp.maximum(nrm, 1e-12)
    k2 = k * (1.0 + (a - 1.0) * ka_ref[...])
    r_out[...] = r
    k_out[...] = k2
    v_out[...] = v
    kk_out[...] = kk
    kka_out[...] = kk * a
    bonus_out[...] = _head_sum(r * k2 * rk_ref[...], ones_bd) * v


def _rwkv_prep(prw, bound, prm, tm):
    n = prw.shape[0]
    row = lambda width: pl.BlockSpec((1, width), lambda i: (0, 0))
    tile = pl.BlockSpec((tm, RWKV_WIDTH), lambda i: (i, 0))
    return pl.pallas_call(
        _rwkv_prep_kernel,
        grid=(n // tm,),
        in_specs=[
            pl.BlockSpec((tm, RWKV_PROJ), lambda i: (i, 0)),
            pl.BlockSpec((1, 1, RWKV_PROJ), lambda i: (i, 0, 0)),
            row(RWKV_PROJ), row(RWKV_WIDTH), row(RWKV_WIDTH),
            pl.BlockSpec((LORA_WIDTH, 3 * RWKV_WIDTH), lambda i: (0, 0)),
            row(RWKV_WIDTH), row(RWKV_WIDTH), row(RWKV_WIDTH),
            pl.BlockSpec((RWKV_WIDTH, RWKV_WIDTH), lambda i: (0, 0)),
        ],
        out_specs=[tile] * 8,
        out_shape=[jax.ShapeDtypeStruct((n, RWKV_WIDTH), F32)] * 8,
        compiler_params=_params(("parallel",)),
        name="rwkv_prep",
    )(prw, bound, prm["mu"], prm["w0"], prm["a0"], prm["lora"], prm["k_k"], prm["k_a"], prm["r_k"],
      prm["ones_bd"])


def _wkv_units(units, s_list, tri, bd_strict, bd_incl, eye2, eye_s, lane_lo):
    seq = units[0][0].shape[0]
    two = 2 * seq
    n = len(units)
    rng = range(n)

    def stack(z):
        return jnp.concatenate([jnp.where(lane_lo, z, 0.0), jnp.where(lane_lo, 0.0, z)], axis=0)

    cl = []
    for (r, lw, k, v, kk, kka) in units:
        hi, mid, lo = _split3(lw)
        cl.append(_dot(tri, hi) + _dot(tri, mid) + _dot(tri, lo))
    gam = [jnp.exp(c) for c in cl]
    gam_inv = [jnp.exp(-c) for c in cl]
    gam_prev = [jnp.exp(cl[i] - units[i][1]) for i in rng]
    gam_last = [g[seq - 1:seq, :] for g in gam]
    a_t = [-units[i][4] * gam_prev[i] for i in rng]
    r_t = [units[i][0] * gam[i] for i in rng]
    b_t = [units[i][5] * gam_inv[i] for i in rng]
    k_t = [units[i][2] * gam_inv[i] for i in rng]
    a_st = [stack(z) for z in a_t]
    r_st = [stack(z) for z in r_t]
    v_st_b = [stack(u[3]).astype(BF16) for u in units]
    bh_st = [stack(b_t[i] * gam_last[i]).astype(BF16) for i in rng]
    kh_st = [stack(k_t[i] * gam_last[i]).astype(BF16) for i in rng]
    x = [_dot_nt(jnp.concatenate([a_st[i], r_st[i]], axis=0).astype(BF16),
                 jnp.concatenate([b_t[i], b_t[i], k_t[i], k_t[i]], axis=0).astype(BF16)) for i in rng]
    m_ab = [jnp.where(bd_strict, z[:two, :two], 0.0) for z in x]
    m_akv = [_dot(jnp.where(bd_strict, x[i][:two, two:], 0.0).astype(BF16), v_st_b[i]) for i in rng]
    n_cat = [jnp.concatenate([jnp.where(bd_incl, z[two:, :two], 0.0),
                              jnp.where(bd_incl, z[two:, two:], 0.0)], axis=1).astype(BF16) for z in x]
    t_inv = [eye2 + m for m in m_ab]
    pw = m_ab
    steps = 1
    while steps * 2 < seq:
        pw_b = [p.astype(BF16) for p in pw]
        pw = [_dot(p, p) for p in pw_b]
        t_inv = [_dot(t_inv[i].astype(BF16), (eye2 + pw[i]).astype(BF16)) for i in rng]
        steps *= 2
    ty = [_dot(t_inv[i].astype(BF16), jnp.concatenate([a_st[i], m_akv[i]], axis=1).astype(BF16))
          for i in rng]
    ah_b = [z[:, :PAIR_LANES].astype(BF16) for z in ty]
    uh_b = [z[:, PAIR_LANES:].astype(BF16) for z in ty]
    p_t = [_dot_tn(bh_st[i], ah_b[i]) + jnp.where(eye_s, gam_last[i], 0.0) for i in rng]
    q_t = [_dot_tn(jnp.concatenate([bh_st[i], kh_st[i]], axis=0),
                   jnp.concatenate([uh_b[i], v_st_b[i]], axis=0)) for i in rng]
    ro = [_dot(n_cat[i], jnp.concatenate(
        [jnp.concatenate([ah_b[i], uh_b[i]], axis=1),
         jnp.concatenate([jnp.zeros_like(v_st_b[i]), v_st_b[i]], axis=1)], axis=0)) for i in rng]
    outs, states = [], []
    for i in rng:
        s_hi, s_lo = _split2(s_list[i])
        rh_hi, rh_lo = _split2(r_st[i] + ro[i][:, :PAIR_LANES])
        o_st = _dot(rh_hi, s_hi) + _dot(rh_hi, s_lo) + _dot(rh_lo, s_hi) + ro[i][:, PAIR_LANES:]
        outs.append(o_st[:seq] + o_st[seq:])
        p_hi, p_lo = _split2(p_t[i])
        states.append(_dot(p_hi, s_hi) + _dot(p_hi, s_lo) + _dot(p_lo, s_hi) + q_t[i])
    return outs, states


def _wkv_kernel(r_ref, lw_ref, k_ref, v_ref, kk_ref, kka_ref, s0_ref, o_ref, s1_ref, s_scr):
    j = pl.program_id(1)
    nb, seq = r_ref.shape[0], r_ref.shape[1]
    two = 2 * seq

    @pl.when(j == 0)
    def _():
        s_scr[...] = s0_ref[...]

    ri = lax.broadcasted_iota(jnp.int32, (seq, seq), 0)
    ci = lax.broadcasted_iota(jnp.int32, (seq, seq), 1)
    tri = (ri >= ci).astype(BF16)
    r2 = lax.broadcasted_iota(jnp.int32, (two, two), 0)
    c2 = lax.broadcasted_iota(jnp.int32, (two, two), 1)
    same = (r2 // seq) == (c2 // seq)
    bd_strict = same & ((r2 % seq) > (c2 % seq))
    bd_incl = same & ((r2 % seq) >= (c2 % seq))
    eye2 = (r2 == c2).astype(F32)
    rs = lax.broadcasted_iota(jnp.int32, (PAIR_LANES, PAIR_LANES), 0)
    cs = lax.broadcasted_iota(jnp.int32, (PAIR_LANES, PAIR_LANES), 1)
    eye_s = rs == cs
    lane_lo = lax.broadcasted_iota(jnp.int32, (seq, PAIR_LANES), 1) < HEAD_DIM

    ids = [(b, pair) for b in range(nb) for pair in range(HEAD_PAIRS)]
    lanes = lambda pair: slice(pair * PAIR_LANES, (pair + 1) * PAIR_LANES)
    units = [tuple(ref[b, :, lanes(pair)] for ref in (r_ref, lw_ref, k_ref, v_ref, kk_ref, kka_ref))
             for (b, pair) in ids]
    s_list = [s_scr[b, pair] for (b, pair) in ids]
    outs, states = _wkv_units(units, s_list, tri, bd_strict, bd_incl, eye2, eye_s, lane_lo)
    for (b, pair), o, s_new in zip(ids, outs, states):
        o_ref[b, :, lanes(pair)] = o
        s_scr[b, pair] = s_new

    @pl.when(j == pl.num_programs(1) - 1)
    def _():
        s1_ref[...] = s_scr[...]


WKV_BATCH_PER_STEP = 2


def _wkv(r, lw, k, v, kk, kka, s0_bd):
    bsz, t_len, _ = r.shape
    nc = t_len // WKV_CHUNK
    nb = WKV_BATCH_PER_STEP
    tile = pl.BlockSpec((nb, WKV_CHUNK, RWKV_WIDTH), lambda b, j: (b, j, 0))
    st = pl.BlockSpec((nb, HEAD_PAIRS, PAIR_LANES, PAIR_LANES), lambda b, j: (b, 0, 0, 0))
    return pl.pallas_call(
        _wkv_kernel,
        grid=(bsz // nb, nc),
        in_specs=[tile] * 6 + [st],
        out_specs=[tile, st],
        out_shape=[
            jax.ShapeDtypeStruct((bsz, t_len, RWKV_WIDTH), F32),
            jax.ShapeDtypeStruct((bsz, HEAD_PAIRS, PAIR_LANES, PAIR_LANES), F32),
        ],
        scratch_shapes=[pltpu.VMEM((nb, HEAD_PAIRS, PAIR_LANES, PAIR_LANES), F32)],
        compiler_params=_params(("parallel", "arbitrary")),
        name="wkv",
    )(r, lw, k, v, kk, kka, s0_bd)


S5_LANE_BLOCKS = S5_WIDTH // 128
S5_BLOCK_GROUPS = 128 // S5_CH
S5_BLOCK_STATE = S5_BLOCK_GROUPS * S5_STATE
S5_ROWS_PER_STEP = 256


def _s5_kernel(u_ref, x0_ref, wst_ref, wy_ref, tmat_ref, dec_ref, y_ref, x1_ref, z_scr, x_scr, *, nc):
    r = pl.program_id(1)
    rows = u_ref.shape[0]
    half = S5_BLOCK_STATE
    u2 = jnp.concatenate([u_ref[:, s, :] for s in range(S5_CHUNK)], axis=1).astype(BF16)
    z_scr[...] = _dot(u2, wst_ref[...])
    dre = dec_ref[:, :half]
    dim = dec_ref[:, half:]
    row0 = r * rows

    def body(i, carry):
        xre, xim = carry
        row = row0 + i
        b = row // nc
        start = (row % nc) == 0
        x0 = x0_ref[pl.ds(b, 1), :]
        xre = jnp.where(start, x0[:, :half], xre)
        xim = jnp.where(start, x0[:, half:], xim)
        zrow = z_scr[pl.ds(i, 1), :]
        z_scr[pl.ds(i, 1), :] = jnp.concatenate([xre, xim], axis=1)
        nre = dre * xre - dim * xim + zrow[:, :half]
        nim = dre * xim + dim * xre + zrow[:, half:]
        x1_ref[pl.ds(b, 1), :] = jnp.concatenate([nre, nim], axis=1)
        return nre, nim

    fin = lax.fori_loop(0, rows, body, (x_scr[0:1, :], x_scr[1:2, :]))
    x_scr[0:1, :] = fin[0]
    x_scr[1:2, :] = fin[1]
    y = _dot(u2, tmat_ref[...]) + _dot(z_scr[...].astype(BF16), wy_ref[...])
    for t in range(S5_CHUNK):
        y_ref[:, t, :] = y[:, t * 128:(t + 1) * 128]


def _s5(u3, x0, mats, nc):
    rows_all = u3.shape[0]
    bsz = rows_all // nc
    rows = min(S5_ROWS_PER_STEP, rows_all)
    width = S5_CHUNK * 128
    state = 2 * S5_BLOCK_STATE
    const = lambda shape: pl.BlockSpec((None,) + shape, lambda j, r: (j, 0, 0))
    tile = pl.BlockSpec((rows, S5_CHUNK, 128), lambda j, r: (r, 0, j))
    return pl.pallas_call(
        functools.partial(_s5_kernel, nc=nc),
        grid=(S5_LANE_BLOCKS, rows_all // rows),
        in_specs=[tile, const((bsz, state)), const((width, state)), const((state, width)),
                  const((width, width)), const((1, state))],
        out_specs=[tile, const((bsz, state))],
        out_shape=[jax.ShapeDtypeStruct(u3.shape, F32),
                   jax.ShapeDtypeStruct((S5_LANE_BLOCKS, bsz, state), F32)],
        scratch_shapes=[pltpu.VMEM((rows, state), F32), pltpu.VMEM((8, S5_BLOCK_STATE), F32)],
        compiler_params=_params(("parallel", "arbitrary")),
        name="s5",
    )(u3, x0, mats["wst"], mats["wy"], mats["tmat"], mats["dec"])


def _s5_matrices(a_re, a_im, log_dt, b_re, b_im, c_re, c_im, d):
    L = S5_CHUNK
    dt = jnp.exp(log_dt)[:, None]
    n = jnp.arange(L + 1, dtype=F32)[:, None, None]
    mag = jnp.exp(a_re * dt * n)
    ang = a_im * dt * n
    pw_re, pw_im = mag * jnp.cos(ang), mag * jnp.sin(ang)
    num_re, num_im = pw_re[1] - 1.0, pw_im[1]
    den = a_re * a_re + a_im * a_im
    f_re = (num_re * a_re + num_im * a_im) / den
    f_im = (num_im * a_re - num_re * a_im) / den
    bb_re = f_re[..., None] * b_re - f_im[..., None] * b_im
    bb_im = f_re[..., None] * b_im + f_im[..., None] * b_re
    pb_re = pw_re[..., None] * bb_re - pw_im[..., None] * bb_im
    pb_im = pw_re[..., None] * bb_im + pw_im[..., None] * bb_re
    kern = (jnp.einsum('gcp,ngpd->ngcd', c_re, pb_re[:L]) - jnp.einsum('gcp,ngpd->ngcd', c_im, pb_im[:L]))
    s_idx = jnp.arange(L)[:, None]
    t_idx = jnp.arange(L)[None, :]
    tk = kern[jnp.clip(t_idx - s_idx, 0, L - 1)]
    tk = jnp.where((t_idx >= s_idx)[:, :, None, None, None], tk, 0.0)
    skip = (jnp.eye(L, dtype=F32)[:, :, None, None, None] * d.reshape(1, 1, S5_GROUPS, S5_CH, 1)
            * jnp.eye(S5_CH, dtype=F32)[None, None, None])
    tk = tk + skip
    nb, bg = S5_LANE_BLOCKS, S5_BLOCK_GROUPS

    def block_diag(compact, row_inner, outer, inner):
        src = jnp.arange(outer * inner)
        dst = jnp.arange(outer * bg * inner)
        pick = ((src[:, None] // inner == dst[None, :] // (bg * inner))
                & (src[:, None] % inner == dst[None, :] % inner)).astype(BF16)
        wide = jnp.einsum('jrk,kn->jrn', compact.astype(BF16), pick, preferred_element_type=F32)
        row_group = (jnp.arange(compact.shape[1]) // row_inner) % bg
        keep = row_group[:, None] == (dst[None, :] // inner) % bg
        return jnp.where(keep, wide, 0.0).astype(BF16)

    tk = tk.reshape(L, L, nb, bg, S5_CH, S5_CH).transpose(2, 0, 3, 5, 1, 4)
    tmat = block_diag(tk.reshape(nb, L * 128, L * S5_CH), S5_CH, L, S5_CH)
    inj = jnp.stack([pb_re[:L][::-1], pb_im[:L][::-1]], axis=0)
    inj = inj.reshape(2, L, nb, bg, S5_STATE, S5_CH).transpose(2, 1, 3, 5, 0, 4)
    wst = block_diag(inj.reshape(nb, L * 128, 2 * S5_STATE), S5_CH, 2, S5_STATE)
    cp_re = c_re[None] * pw_re[1:, :, None, :] - c_im[None] * pw_im[1:, :, None, :]
    cp_im = c_re[None] * pw_im[1:, :, None, :] + c_im[None] * pw_re[1:, :, None, :]
    rd = jnp.stack([cp_re, -cp_im], axis=0).reshape(2, L, nb, bg, S5_CH, S5_STATE)
    rd = rd.transpose(2, 0, 3, 5, 1, 4)
    wy = block_diag(rd.reshape(nb, 2 * bg * S5_STATE, L * S5_CH), S5_STATE, L, S5_CH)
    dec = jnp.concatenate([pw_re[L].reshape(nb, 1, bg * S5_STATE), pw_im[L].reshape(nb, 1, bg * S5_STATE)],
                          axis=2)
    return {"wst": wst.astype(BF16), "wy": wy.astype(BF16), "tmat": tmat.astype(BF16), "dec": dec}


def _mix_out_kernel(o_ref, bonus_ref, g_ref, y_ref, x_ref, gnw_ref, gnb_ref, wglu_ref, bglu_ref,
                    wout_ref, ones_ref, x1_ref):
    ones_bd = ones_ref[...]
    o = o_ref[...]
    mu = _head_sum(o, ones_bd) * (1.0 / HEAD_DIM)
    d = o - mu
    var = _head_sum(d * d, ones_bd) * (1.0 / HEAD_DIM)
    o = d * lax.rsqrt(var + GN_EPS) * gnw_ref[...] + gnb_ref[...]
    o = (o + bonus_ref[...]) * g_ref[...]
    y = _gelu(y_ref[...])
    y = y * _sigmoid(_dot(y.astype(BF16), wglu_ref[...]) + bglu_ref[...])
    mixed = jnp.concatenate([o, y], axis=1).astype(BF16)
    x1_ref[...] = x_ref[...] + _dot(mixed, wout_ref[...])


def _mix_out(o, bonus, g, y, x2d, prm, tm):
    n = x2d.shape[0]
    tile = pl.BlockSpec((tm, RWKV_WIDTH), lambda i: (i, 0))
    row = pl.BlockSpec((1, RWKV_WIDTH), lambda i: (0, 0))
    sq = pl.BlockSpec((RWKV_WIDTH, RWKV_WIDTH), lambda i: (0, 0))
    return pl.pallas_call(
        _mix_out_kernel,
        grid=(n // tm,),
        in_specs=[tile, tile, tile, tile, pl.BlockSpec((tm, D_MODEL), lambda i: (i, 0)),
                  row, row, sq, row, pl.BlockSpec((D_MODEL, D_MODEL), lambda i: (0, 0)), sq],
        out_specs=pl.BlockSpec((tm, D_MODEL), lambda i: (i, 0)),
        out_shape=jax.ShapeDtypeStruct((n, D_MODEL), F32),
        compiler_params=_params(("parallel",)),
        name="mix_out",
    )(o, bonus, g, y, x2d, prm["gn_w"], prm["gn_b"], prm["w_glu"], prm["b_glu"], prm["w_out"],
      prm["ones_bd"])


def _sort16_pairs():
    def merge(lo, hi, r):
        step = 2 * r
        if step < hi - lo:
            yield from merge(lo, hi, step)
            yield from merge(lo + r, hi, step)
            yield from ((i, i + r) for i in range(lo + r, hi - r, step))
        else:
            yield (lo, lo + r)

    def sort(lo, hi):
        if hi > lo:
            mid = lo + (hi - lo) // 2
            yield from sort(lo, mid)
            yield from sort(mid + 1, hi)
            yield from merge(lo, hi, 1)

    return list(sort(0, PEER_TOPK - 1))


def _top16(x):
    sub = x.shape[0] // PEER_TOPK
    v = [x[k * sub:(k + 1) * sub, :] for k in range(PEER_TOPK)]
    for i, j in _sort16_pairs():
        v[i], v[j] = jnp.maximum(v[i], v[j]), jnp.minimum(v[i], v[j])
    vals = []
    for q in range(PEER_TOPK):
        m = jnp.max(v[0], axis=0, keepdims=True)
        vals.append(m)
        if q + 1 < PEER_TOPK:
            hit = v[0] == m
            for k in range(PEER_TOPK - 1 - q):
                v[k] = jnp.where(hit, v[k + 1], v[k])
    return vals


def _peer_route_kernel(x_ref, g_ref, wqt_ref, k1_ref, k2_ref,
                       h2t_ref, e1_ref, cnt_ref, rank2_ref, e2_ref):
    h2 = _rmsnorm(x_ref[...], g_ref[...])
    h2_b = h2.astype(BF16)
    h2t_ref[...] = jnp.transpose(h2).astype(BF16)
    qt = _dot_nt(wqt_ref[...], h2_b)
    for h in range(PEER_HEADS):
        q1 = qt[h * PEER_QDIM:h * PEER_QDIM + PEER_HALF].astype(BF16)
        q2 = qt[h * PEER_QDIM + PEER_HALF:(h + 1) * PEER_QDIM].astype(BF16)
        s1 = _dot(k1_ref[h], q1)
        s2 = _dot(k2_ref[h], q2)
        a1 = _top16(s1)
        a2 = _top16(s2)
        a1_mat = jnp.concatenate(a1, axis=0)
        a2_lo = jnp.concatenate(a2[:8], axis=0)
        cand = [a1[0] + a2_lo, a1[0] + jnp.concatenate(a2[8:], axis=0)]
        cand += [a1[i] + a2_lo for i in range(1, 8)]
        cand.append(a1_mat[8:] + a2[0])
        cand = jnp.concatenate(cand, axis=0)
        top = a1[0] + a2[0]
        zsum = jnp.zeros_like(top)
        thr = top
        for _ in range(PEER_TOPK):
            thr = jnp.max(cand, axis=0, keepdims=True)
            zsum = zsum + jnp.exp(thr - top)
            cand = jnp.where(cand == thr, -jnp.inf, cand)
        cnt = jnp.zeros_like(s1)
        for q in range(PEER_TOPK):
            reach = (a1_mat + a2[q]) >= thr
            b_q = jnp.min(jnp.where(reach, a1_mat, jnp.inf), axis=0, keepdims=True)
            cnt = jnp.where(s1 >= b_q, float(q + 1), cnt)
        rank2 = jnp.full(s2.shape, float(PEER_TOPK), F32)
        for q in reversed(range(PEER_TOPK)):
            rank2 = jnp.where(s2 >= a2[q], float(q), rank2)
        e1_ref[h] = jnp.exp(s1 - a1[0])
        cnt_ref[h] = cnt
        rank2_ref[h] = rank2.astype(BF16)
        e2_ref[h] = (jnp.exp(s2 - a2[0]) / zsum).astype(BF16)


def _peer_route(x1, ln2_g, wqt_bf, k1_bf, k2_bf, tt):
    n = x1.shape[0]
    keys = pl.BlockSpec((PEER_HEADS, N_KEYS, PEER_HALF), lambda i: (0, 0, 0))
    tok = pl.BlockSpec((PEER_HEADS, N_KEYS, tt), lambda i: (0, 0, i))
    per_key = lambda dt: jax.ShapeDtypeStruct((PEER_HEADS, N_KEYS, n), dt)
    return pl.pallas_call(
        _peer_route_kernel,
        grid=(n // tt,),
        in_specs=[
            pl.BlockSpec((tt, D_MODEL), lambda i: (i, 0)),
            pl.BlockSpec((1, D_MODEL), lambda i: (0, 0)),
            pl.BlockSpec((PEER_HEADS * PEER_QDIM, D_MODEL), lambda i: (0, 0)),
            keys, keys,
        ],
        out_specs=[pl.BlockSpec((D_MODEL, tt), lambda i: (0, i)), tok, tok, tok, tok],
        out_shape=[jax.ShapeDtypeStruct((D_MODEL, n), BF16), per_key(F32), per_key(F32), per_key(BF16),
                   per_key(BF16)],
        compiler_params=_params(("parallel",)),
        name="peer_route",
    )(x1, ln2_g, wqt_bf, k1_bf, k2_bf)


PEER_CHUNK = 1024
PEER_CHUNK_KEYS = PEER_CHUNK // N_KEYS
PEER_PIECE_ROWS = 16
PEER_PIECE_LANES = 128


def _peer_dense_kernel(h2t_ref, u_ref, vt_ref, e1_ref, cnt_ref, rank2_in_ref, e2_in_ref, x1_ref, g_ref,
                       y_ref, acc_scr, a_scr, w_scr, tab_scr):
    c = pl.program_id(1)
    tt = h2t_ref.shape[1]

    @pl.when(c == 0)
    def _():
        acc_scr[...] = jnp.zeros_like(acc_scr)
        for sub in range(N_KEYS // PEER_PIECE_ROWS):
            second = slice(sub * PEER_PIECE_ROWS, (sub + 1) * PEER_PIECE_ROWS)
            for blk in range(tt // PEER_PIECE_LANES):
                lanes = slice(blk * PEER_PIECE_LANES, (blk + 1) * PEER_PIECE_LANES)
                for h in range(PEER_HEADS):
                    tab_scr[sub, blk, h, 0] = rank2_in_ref[h, second, lanes]
                    tab_scr[sub, blk, h, 1] = e2_in_ref[h, second, lanes]

    a_scr[...] = _dot(u_ref[...], h2t_ref[...])

    def build(key, carry):
        base = pl.multiple_of(key * N_KEYS, N_KEYS)
        cnt_rows = [cnt_ref[h, pl.ds(key, 1), :] for h in range(PEER_HEADS)]
        e1_rows = [e1_ref[h, pl.ds(key, 1), :] for h in range(PEER_HEADS)]
        for blk in range(tt // PEER_PIECE_LANES):
            lanes = slice(blk * PEER_PIECE_LANES, (blk + 1) * PEER_PIECE_LANES)
            tile = (PEER_PIECE_ROWS, PEER_PIECE_LANES)
            cnt_b = [jnp.broadcast_to(cnt_rows[h][:, lanes], tile).astype(BF16) for h in range(PEER_HEADS)]
            e1_b = [jnp.broadcast_to(e1_rows[h][:, lanes], tile).astype(BF16) for h in range(PEER_HEADS)]
            for sub in range(N_KEYS // PEER_PIECE_ROWS):
                gate = None
                for h in range(PEER_HEADS):
                    sel = jnp.where(tab_scr[sub, blk, h, 0] < cnt_b[h], tab_scr[sub, blk, h, 1],
                                    jnp.zeros(tile, BF16))
                    term = sel * e1_b[h]
                    gate = term if gate is None else gate + term
                rows = pl.ds(base + sub * PEER_PIECE_ROWS, PEER_PIECE_ROWS)
                w_scr[rows, lanes] = _gelu(a_scr[rows, lanes]).astype(BF16) * gate
        return carry

    lax.fori_loop(0, PEER_CHUNK_KEYS, build, 0)
    acc_scr[...] += _dot(vt_ref[...], w_scr[...])

    @pl.when(c == pl.num_programs(1) - 1)
    def _():
        x2 = x1_ref[...] + jnp.transpose(acc_scr[...])
        y_ref[...] = _rmsnorm(x2, g_ref[...])


def _peer_dense(h2t, u_bf, vt_bf, e1, cnt, rank2, e2, x1, lnf_g, tt):
    n = x1.shape[0]
    nchunks = N_EXPERTS // PEER_CHUNK
    by_chunk = lambda z: z.reshape(PEER_HEADS, nchunks, PEER_CHUNK_KEYS, n)
    tok_all = pl.BlockSpec((PEER_HEADS, N_KEYS, tt), lambda i, c: (0, 0, i))
    tok_chunk = pl.BlockSpec((PEER_HEADS, None, PEER_CHUNK_KEYS, tt), lambda i, c: (0, c, 0, i))
    return pl.pallas_call(
        _peer_dense_kernel,
        grid=(n // tt, nchunks),
        in_specs=[
            pl.BlockSpec((D_MODEL, tt), lambda i, c: (0, i)),
            pl.BlockSpec((PEER_CHUNK, D_MODEL), lambda i, c: (c, 0)),
            pl.BlockSpec((None, D_MODEL, PEER_CHUNK), lambda i, c: (c, 0, 0)),
            tok_chunk, tok_chunk, tok_all, tok_all,
            pl.BlockSpec((tt, D_MODEL), lambda i, c: (i, 0)),
            pl.BlockSpec((1, D_MODEL), lambda i, c: (0, 0)),
        ],
        out_specs=pl.BlockSpec((tt, D_MODEL), lambda i, c: (i, 0)),
        out_shape=jax.ShapeDtypeStruct((n, D_MODEL), F32),
        scratch_shapes=[pltpu.VMEM((D_MODEL, tt), F32), pltpu.VMEM((PEER_CHUNK, tt), F32),
                        pltpu.VMEM((PEER_CHUNK, tt), BF16),
                        pltpu.VMEM((N_KEYS // PEER_PIECE_ROWS, tt // PEER_PIECE_LANES, PEER_HEADS, 2,
                                    PEER_PIECE_ROWS, PEER_PIECE_LANES), BF16)],
        compiler_params=_params(("parallel", "arbitrary")),
        name="peer_dense",
    )(h2t, u_bf, vt_bf, by_chunk(e1), by_chunk(cnt), rank2, e2, x1, lnf_g)


def _pick_tile(n, target):
    t = min(n, target)
    while n % t:
        t //= 2
    return t


def _layer(x, wkv0, shift0, s5re0, s5im0, prm, s5m, lnf_g):
    bsz, t_len, _ = x.shape
    n = bsz * t_len
    x2d = x.reshape(n, D_MODEL)
    tm = _pick_tile(t_len, 512)
    prw, u = _in_proj(x2d, prm["ln1_g"], prm["w_in"], tm)

    tiles_per_seq = t_len // tm
    prw3 = prw.reshape(bsz, t_len, RWKV_PROJ)
    last_rows = prw3[:, tm - 1::tm, :]
    bound = jnp.concatenate([shift0, last_rows[:, :tiles_per_seq - 1, :]], axis=1)
    bound = bound.reshape(bsz * tiles_per_seq, 1, RWKV_PROJ)
    r, lw, k, v, kk, kka, g, bonus = _rwkv_prep(prw, bound, prm, tm)

    t_pad = -(-t_len // WKV_CHUNK) * WKV_CHUNK
    seq3 = lambda z: jnp.pad(z.reshape(bsz, t_len, RWKV_WIDTH), ((0, 0), (0, t_pad - t_len), (0, 0)))
    st = wkv0.reshape(bsz, HEAD_PAIRS, 2, HEAD_DIM, HEAD_DIM).transpose(0, 1, 2, 4, 3)
    eye2 = jnp.eye(2, dtype=F32)
    s0_bd = jnp.einsum('bphkv,hg->bphkgv', st, eye2).reshape(bsz, HEAD_PAIRS, PAIR_LANES, PAIR_LANES)
    o, s1_bd = _wkv(seq3(r), seq3(lw), seq3(k), seq3(v), seq3(kk), seq3(kka), s0_bd)
    o = o[:, :t_len].reshape(n, RWKV_WIDTH)
    s1 = s1_bd.reshape(bsz, HEAD_PAIRS, 2, HEAD_DIM, 2, HEAD_DIM)
    s1 = jnp.stack([s1[:, :, 0, :, 0, :], s1[:, :, 1, :, 1, :]], axis=2)
    wkv1 = s1.transpose(0, 1, 2, 4, 3).reshape(bsz, RWKV_HEADS, HEAD_DIM, HEAD_DIM)

    nc = t_len // S5_CHUNK
    blk = lambda z: z.reshape(bsz, S5_LANE_BLOCKS, S5_BLOCK_STATE).transpose(1, 0, 2)
    x0 = jnp.concatenate([blk(s5re0), blk(s5im0)], axis=2)
    y3, x1s = _s5(u.reshape(bsz * nc, S5_CHUNK, S5_WIDTH), x0, s5m, nc)
    y = y3.reshape(n, S5_WIDTH)
    unblk = lambda z: z.transpose(1, 0, 2).reshape(bsz, S5_GROUPS, S5_STATE)
    s5re1, s5im1 = unblk(x1s[:, :, :S5_BLOCK_STATE]), unblk(x1s[:, :, S5_BLOCK_STATE:])

    x1 = _mix_out(o, bonus, g, y, x2d, prm, tm)

    tt = _pick_tile(n, 512)
    h2t, e1, cnt, rank2, e2 = _peer_route(x1, prm["ln2_g"], prm["w_qt"], prm["keys1"], prm["keys2"], tt)
    yout = _peer_dense(h2t, prm["peer_u"], prm["peer_vt"], e1, cnt, rank2, e2, x1, lnf_g, tt)
    shift1 = prw3[:, -1:, :]
    return yout.reshape(bsz, t_len, D_MODEL), wkv1, shift1, s5re1, s5im1


def kernel(x_prompt, x_sample, state_wkv, state_shift, state_s5_re, state_s5_im, ln1_g, w_in, rwkv_mu, rwkv_w0, rwkv_w2, rwkv_a0, rwkv_a2, rwkv_g2, rwkv_k_k, rwkv_k_a, rwkv_r_k, rwkv_gn_w, rwkv_gn_b, s5_a_re, s5_a_im, s5_log_dt, s5_b_re, s5_b_im, s5_c_re, s5_c_im, s5_d, s5_w_glu, s5_b_glu, w_out, ln2_g, peer_w_q, peer_keys1, peer_keys2, peer_u, peer_v, lnf_g):
    depth = w_in.shape[0]
    assert depth == 1
    l = 0
    row = lambda z: z.reshape(1, -1).astype(F32)
    lora = jnp.zeros((LORA_WIDTH, 3 * RWKV_WIDTH), F32)
    lora = lora.at[:64, :RWKV_WIDTH].set(rwkv_w2[l])
    lora = lora.at[64:128, RWKV_WIDTH:2 * RWKV_WIDTH].set(rwkv_a2[l])
    lora = lora.at[128:, 2 * RWKV_WIDTH:].set(rwkv_g2[l])
    lane = jnp.arange(RWKV_WIDTH)
    ones_bd = (lane[:, None] // HEAD_DIM == lane[None, :] // HEAD_DIM).astype(BF16)
    prm = {
        "ln1_g": row(ln1_g[l]), "w_in": w_in[l].astype(BF16), "mu": row(rwkv_mu[l]),
        "w0": row(rwkv_w0[l]), "a0": row(rwkv_a0[l]), "lora": lora.astype(BF16),
        "k_k": row(rwkv_k_k[l]), "k_a": row(rwkv_k_a[l]), "r_k": row(rwkv_r_k[l]),
        "gn_w": row(rwkv_gn_w[l]), "gn_b": row(rwkv_gn_b[l]), "ones_bd": ones_bd,
        "w_glu": s5_w_glu[l].astype(BF16), "b_glu": row(s5_b_glu[l]), "w_out": w_out[l].astype(BF16),
        "ln2_g": row(ln2_g[l]), "w_qt": peer_w_q[l].T.astype(BF16),
        "keys1": peer_keys1[l].astype(BF16), "keys2": peer_keys2[l].astype(BF16),
        "peer_u": peer_u[l].astype(BF16),
        "peer_vt": peer_v[l].astype(BF16).reshape(N_EXPERTS // PEER_CHUNK, PEER_CHUNK, D_MODEL)
        .transpose(0, 2, 1),
    }
    s5m = _s5_matrices(s5_a_re[l], s5_a_im[l], s5_log_dt[l], s5_b_re[l], s5_b_im[l], s5_c_re[l],
                       s5_c_im[l], s5_d[l])
    lnf = row(lnf_g)
    bp = x_prompt.shape[0]
    zeros = lambda *s: jnp.zeros(s, F32)
    yp, a1, a2, a3, a4 = _layer(
        x_prompt, zeros(bp, RWKV_HEADS, HEAD_DIM, HEAD_DIM), zeros(bp, 1, RWKV_PROJ),
        zeros(bp, S5_GROUPS, S5_STATE), zeros(bp, S5_GROUPS, S5_STATE), prm, s5m, lnf)
    ys, b1, b2, b3, b4 = _layer(x_sample, state_wkv[l], state_shift[l], state_s5_re[l], state_s5_im[l],
                                prm, s5m, lnf)
    st = lambda z: z[None]
    return (yp, ys, st(a1), st(a2), st(a3), st(a4), st(b1), st(b2), st(b3), st(b4))
```

```python
import functools
import math

import jax
import jax.numpy as jnp
from jax import lax
from jax.experimental import pallas as pl
from jax.experimental.pallas import tpu as pltpu

F32 = jnp.float32
BF16 = jnp.bfloat16

D_MODEL = 1024
RWKV_WIDTH = 512
HEAD_DIM = 64
RWKV_HEADS = 8
HEAD_PAIRS = RWKV_HEADS // 2
PAIR_LANES = 2 * HEAD_DIM
LORA_WIDTH = 256
RWKV_PROJ = 3 * RWKV_WIDTH + LORA_WIDTH
S5_WIDTH = 512
S5_CH = 16
S5_GROUPS = 32
S5_STATE = 64
S5_CHUNK = 16
IN_PROJ = RWKV_PROJ + S5_WIDTH
PEER_HEADS = 8
N_KEYS = 128
N_EXPERTS = N_KEYS * N_KEYS
PEER_TOPK = 16
PEER_QDIM = 256
PEER_HALF = 128
NORM_EPS = 1e-6
GN_EPS = HEAD_DIM * 1e-5
WKV_CHUNK = 64

VMEM_LIMIT_BYTES = 48 * 1024 * 1024


def _params(semantics, flags=None):
    return pltpu.CompilerParams(dimension_semantics=semantics, vmem_limit_bytes=VMEM_LIMIT_BYTES,
                                flags=flags)


def _dot(a, b):
    return jnp.dot(a, b, preferred_element_type=F32)


def _dot_nt(a, b):
    return lax.dot_general(a, b, (((1,), (1,)), ((), ())), preferred_element_type=F32)


def _dot_tn(a, b):
    return lax.dot_general(a, b, (((0,), (0,)), ((), ())), preferred_element_type=F32)


def _split2(x):
    hi = x.astype(BF16)
    lo = (x - hi.astype(F32)).astype(BF16)
    return hi, lo


def _split3(x):
    hi = x.astype(BF16)
    r1 = x - hi.astype(F32)
    mid = r1.astype(BF16)
    lo = (r1 - mid.astype(F32)).astype(BF16)
    return hi, mid, lo


def _head_sum(x, ones_bd):
    hi, lo = _split2(x)
    return _dot(hi, ones_bd) + _dot(lo, ones_bd)


def _rmsnorm(x, g):
    return x * lax.rsqrt(jnp.mean(x * x, axis=-1, keepdims=True) + NORM_EPS) * g


def _gelu(x):
    c = math.sqrt(2.0 / math.pi)
    half = 0.5 * x
    return half * jnp.tanh(x * (c + (c * 0.044715) * (x * x))) + half


def _sigmoid(x):
    return 1.0 / (1.0 + jnp.exp(-x))


def _in_proj_kernel(x_ref, g_ref, w_ref, prw_ref, u_ref):
    h = _rmsnorm(x_ref[...], g_ref[...]).astype(BF16)
    p = _dot(h, w_ref[...])
    prw_ref[...] = p[:, :RWKV_PROJ]
    u_ref[...] = p[:, RWKV_PROJ:]


def _in_proj(x2d, ln1_g, w_in_bf, tm):
    n = x2d.shape[0]
    return pl.pallas_call(
        _in_proj_kernel,
        grid=(n // tm,),
        in_specs=[
            pl.BlockSpec((tm, D_MODEL), lambda i: (i, 0)),
            pl.BlockSpec((1, D_MODEL), lambda i: (0, 0)),
            pl.BlockSpec((D_MODEL, IN_PROJ), lambda i: (0, 0)),
        ],
        out_specs=[
            pl.BlockSpec((tm, RWKV_PROJ), lambda i: (i, 0)),
            pl.BlockSpec((tm, S5_WIDTH), lambda i: (i, 0)),
        ],
        out_shape=[
            jax.ShapeDtypeStruct((n, RWKV_PROJ), F32),
            jax.ShapeDtypeStruct((n, S5_WIDTH), F32),
        ],
        compiler_params=_params(("parallel",)),
        name="in_proj",
    )(x2d, ln1_g, w_in_bf)


def _rwkv_prep_kernel(p_ref, bound_ref, mu_ref, w0_ref, a0_ref, lora_ref, kk_ref_, ka_ref, rk_ref,
                      ones_ref, r_out, lw_out, k_out, v_out, kk_out, kka_out, g_out, bonus_out):
    p = p_ref[...]
    tm = p.shape[0]
    row = lax.broadcasted_iota(jnp.int32, p.shape, 0)
    p_prev = jnp.where(row == 0, bound_ref[0], pltpu.roll(p, 1, axis=0))
    m = p + (p_prev - p) * mu_ref[...]
    r = m[:, :RWKV_WIDTH]
    k = m[:, RWKV_WIDTH:2 * RWKV_WIDTH]
    v = m[:, 2 * RWKV_WIDTH:3 * RWKV_WIDTH]
    z = m[:, 3 * RWKV_WIDTH:]
    lane = lax.broadcasted_iota(jnp.int32, (tm, LORA_WIDTH), 1)
    feat = jnp.where(lane < 64, jnp.tanh(z), jnp.where(lane < 128, z, _sigmoid(z)))
    lo = _dot(feat.astype(BF16), lora_ref[...])
    wl = -(w0_ref[...] + lo[:, :RWKV_WIDTH])
    softplus = jnp.maximum(wl, 0.0) + jnp.log(1.0 + jnp.exp(-jnp.abs(wl)))
    lw_out[...] = -jnp.exp(-softplus - 0.5)
    a = _sigmoid(a0_ref[...] + lo[:, RWKV_WIDTH:2 * RWKV_WIDTH])
    g_out[...] = lo[:, 2 * RWKV_WIDTH:]
    ones_bd = ones_ref[...]
    kk = k * kk_ref_[...]
    nrm = jnp.sqrt(_head_sum(kk * kk, ones_bd))
    kk = kk / jnp.maximum(nrm, 1e-12)
    k2 = k * (1.0 + (a - 1.0) * ka_ref[...])
    r_out[...] = r
    k_out[...] = k2
    v_out[...] = v
    kk_out[...] = kk
    kka_out[...] = kk * a
    bonus_out[...] = _head_sum(r * k2 * rk_ref[...], ones_bd) * v


def _rwkv_prep(prw, bound, prm, tm):
    n = prw.shape[0]
    row = lambda width: pl.BlockSpec((1, width), lambda i: (0, 0))
    tile = pl.BlockSpec((tm, RWKV_WIDTH), lambda i: (i, 0))
    return pl.pallas_call(
        _rwkv_prep_kernel,
        grid=(n // tm,),
        in_specs=[
            pl.BlockSpec((tm, RWKV_PROJ), lambda i: (i, 0)),
            pl.BlockSpec((1, 1, RWKV_PROJ), lambda i: (i, 0, 0)),
            row(RWKV_PROJ), row(RWKV_WIDTH), row(RWKV_WIDTH),
            pl.BlockSpec((LORA_WIDTH, 3 * RWKV_WIDTH), lambda i: (0, 0)),
            row(RWKV_WIDTH), row(RWKV_WIDTH), row(RWKV_WIDTH),
            pl.BlockSpec((RWKV_WIDTH, RWKV_WIDTH), lambda i: (0, 0)),
        ],
        out_specs=[tile] * 8,
        out_shape=[jax.ShapeDtypeStruct((n, RWKV_WIDTH), F32)] * 8,
        compiler_params=_params(("parallel",)),
        name="rwkv_prep",
    )(prw, bound, prm["mu"], prm["w0"], prm["a0"], prm["lora"], prm["k_k"], prm["k_a"], prm["r_k"],
      prm["ones_bd"])


def _wkv_units(units, s_list, tri, bd_strict, bd_incl, eye2, eye_s, lane_lo):
    seq = units[0][0].shape[0]
    two = 2 * seq
    n = len(units)
    rng = range(n)

    def stack(z):
        return jnp.concatenate([jnp.where(lane_lo, z, 0.0), jnp.where(lane_lo, 0.0, z)], axis=0)

    cl = []
    for (r, lw, k, v, kk, kka) in units:
        hi, mid, lo = _split3(lw)
        cl.append(_dot(tri, hi) + _dot(tri, mid) + _dot(tri, lo))
    gam = [jnp.exp(c) for c in cl]
    gam_inv = [jnp.exp(-c) for c in cl]
    gam_prev = [jnp.exp(cl[i] - units[i][1]) for i in rng]
    gam_last = [g[seq - 1:seq, :] for g in gam]
    a_t = [-units[i][4] * gam_prev[i] for i in rng]
    r_t = [units[i][0] * gam[i] for i in rng]
    b_t = [units[i][5] * gam_inv[i] for i in rng]
    k_t = [units[i][2] * gam_inv[i] for i in rng]
    a_st = [stack(z) for z in a_t]
    r_st = [stack(z) for z in r_t]
    v_st_b = [stack(u[3]).astype(BF16) for u in units]
    bh_st = [stack(b_t[i] * gam_last[i]).astype(BF16) for i in rng]
    kh_st = [stack(k_t[i] * gam_last[i]).astype(BF16) for i in rng]
    x = [_dot_nt(jnp.concatenate([a_st[i], r_st[i]], axis=0).astype(BF16),
                 jnp.concatenate([b_t[i], b_t[i], k_t[i], k_t[i]], axis=0).astype(BF16)) for i in rng]
    m_ab = [jnp.where(bd_strict, z[:two, :two], 0.0) for z in x]
    m_akv = [_dot(jnp.where(bd_strict, x[i][:two, two:], 0.0).astype(BF16), v_st_b[i]) for i in rng]
    n_cat = [jnp.concatenate([jnp.where(bd_incl, z[two:, :two], 0.0),
                              jnp.where(bd_incl, z[two:, two:], 0.0)], axis=1).astype(BF16) for z in x]
    t_inv = [eye2 + m for m in m_ab]
    pw = m_ab
    steps = 1
    while steps * 2 < seq:
        pw_b = [p.astype(BF16) for p in pw]
        pw = [_dot(p, p) for p in pw_b]
        t_inv = [_dot(t_inv[i].astype(BF16), (eye2 + pw[i]).astype(BF16)) for i in rng]
        steps *= 2
    ty = [_dot(t_inv[i].astype(BF16), jnp.concatenate([a_st[i], m_akv[i]], axis=1).astype(BF16))
          for i in rng]
    ah_b = [z[:, :PAIR_LANES].astype(BF16) for z in ty]
    uh_b = [z[:, PAIR_LANES:].astype(BF16) for z in ty]
    p_t = [_dot_tn(bh_st[i], ah_b[i]) + jnp.where(eye_s, gam_last[i], 0.0) for i in rng]
    q_t = [_dot_tn(jnp.concatenate([bh_st[i], kh_st[i]], axis=0),
                   jnp.concatenate([uh_b[i], v_st_b[i]], axis=0)) for i in rng]
    ro = [_dot(n_cat[i], jnp.concatenate(
        [jnp.concatenate([ah_b[i], uh_b[i]], axis=1),
         jnp.concatenate([jnp.zeros_like(v_st_b[i]), v_st_b[i]], axis=1)], axis=0)) for i in rng]
    outs, states = [], []
    for i in rng:
        s_hi, s_lo = _split2(s_list[i])
        rh_hi, rh_lo = _split2(r_st[i] + ro[i][:, :PAIR_LANES])
        o_st = _dot(rh_hi, s_hi) + _dot(rh_hi, s_lo) + _dot(rh_lo, s_hi) + ro[i][:, PAIR_LANES:]
        outs.append(o_st[:seq] + o_st[seq:])
        p_hi, p_lo = _split2(p_t[i])
        states.append(_dot(p_hi, s_hi) + _dot(p_hi, s_lo) + _dot(p_lo, s_hi) + q_t[i])
    return outs, states


def _wkv_kernel(r_ref, lw_ref, k_ref, v_ref, kk_ref, kka_ref, s0_ref, o_ref, s1_ref, s_scr):
    j = pl.program_id(1)
    nb, seq = r_ref.shape[0], r_ref.shape[1]
    two = 2 * seq

    @pl.when(j == 0)
    def _():
        s_scr[...] = s0_ref[...]

    ri = lax.broadcasted_iota(jnp.int32, (seq, seq), 0)
    ci = lax.broadcasted_iota(jnp.int32, (seq, seq), 1)
    tri = (ri >= ci).astype(BF16)
    r2 = lax.broadcasted_iota(jnp.int32, (two, two), 0)
    c2 = lax.broadcasted_iota(jnp.int32, (two, two), 1)
    same = (r2 // seq) == (c2 // seq)
    bd_strict = same & ((r2 % seq) > (c2 % seq))
    bd_incl = same & ((r2 % seq) >= (c2 % seq))
    eye2 = (r2 == c2).astype(F32)
    rs = lax.broadcasted_iota(jnp.int32, (PAIR_LANES, PAIR_LANES), 0)
    cs = lax.broadcasted_iota(jnp.int32, (PAIR_LANES, PAIR_LANES), 1)
    eye_s = rs == cs
    lane_lo = lax.broadcasted_iota(jnp.int32, (seq, PAIR_LANES), 1) < HEAD_DIM

    ids = [(b, pair) for b in range(nb) for pair in range(HEAD_PAIRS)]
    lanes = lambda pair: slice(pair * PAIR_LANES, (pair + 1) * PAIR_LANES)
    units = [tuple(ref[b, :, lanes(pair)] for ref in (r_ref, lw_ref, k_ref, v_ref, kk_ref, kka_ref))
             for (b, pair) in ids]
    s_list = [s_scr[b, pair] for (b, pair) in ids]
    outs, states = _wkv_units(units, s_list, tri, bd_strict, bd_incl, eye2, eye_s, lane_lo)
    for (b, pair), o, s_new in zip(ids, outs, states):
        o_ref[b, :, lanes(pair)] = o
        s_scr[b, pair] = s_new

    @pl.when(j == pl.num_programs(1) - 1)
    def _():
        s1_ref[...] = s_scr[...]


WKV_BATCH_PER_STEP = 2


def _wkv(r, lw, k, v, kk, kka, s0_bd):
    bsz, t_len, _ = r.shape
    nc = t_len // WKV_CHUNK
    nb = WKV_BATCH_PER_STEP
    tile = pl.BlockSpec((nb, WKV_CHUNK, RWKV_WIDTH), lambda b, j: (b, j, 0))
    st = pl.BlockSpec((nb, HEAD_PAIRS, PAIR_LANES, PAIR_LANES), lambda b, j: (b, 0, 0, 0))
    return pl.pallas_call(
        _wkv_kernel,
        grid=(bsz // nb, nc),
        in_specs=[tile] * 6 + [st],
        out_specs=[tile, st],
        out_shape=[
            jax.ShapeDtypeStruct((bsz, t_len, RWKV_WIDTH), F32),
            jax.ShapeDtypeStruct((bsz, HEAD_PAIRS, PAIR_LANES, PAIR_LANES), F32),
        ],
        scratch_shapes=[pltpu.VMEM((nb, HEAD_PAIRS, PAIR_LANES, PAIR_LANES), F32)],
        compiler_params=_params(("parallel", "arbitrary")),
        name="wkv",
    )(r, lw, k, v, kk, kka, s0_bd)


S5_LANE_BLOCKS = S5_WIDTH // 128
S5_BLOCK_GROUPS = 128 // S5_CH
S5_BLOCK_STATE = S5_BLOCK_GROUPS * S5_STATE
S5_ROWS_PER_STEP = 256


def _s5_kernel(u_ref, x0_ref, wst_ref, wy_ref, tmat_ref, dec_ref, y_ref, x1_ref, z_scr, x_scr, *, nc):
    r = pl.program_id(1)
    rows = u_ref.shape[0]
    half = S5_BLOCK_STATE
    u2 = jnp.concatenate([u_ref[:, s, :] for s in range(S5_CHUNK)], axis=1).astype(BF16)
    z_scr[...] = _dot(u2, wst_ref[...])
    dre = dec_ref[:, :half]
    dim = dec_ref[:, half:]
    row0 = r * rows

    def body(i, carry):
        xre, xim = carry
        row = row0 + i
        b = row // nc
        start = (row % nc) == 0
        x0 = x0_ref[pl.ds(b, 1), :]
        xre = jnp.where(start, x0[:, :half], xre)
        xim = jnp.where(start, x0[:, half:], xim)
        zrow = z_scr[pl.ds(i, 1), :]
        z_scr[pl.ds(i, 1), :] = jnp.concatenate([xre, xim], axis=1)
        nre = dre * xre - dim * xim + zrow[:, :half]
        nim = dre * xim + dim * xre + zrow[:, half:]
        x1_ref[pl.ds(b, 1), :] = jnp.concatenate([nre, nim], axis=1)
        return nre, nim

    fin = lax.fori_loop(0, rows, body, (x_scr[0:1, :], x_scr[1:2, :]))
    x_scr[0:1, :] = fin[0]
    x_scr[1:2, :] = fin[1]
    y = _dot(u2, tmat_ref[...]) + _dot(z_scr[...].astype(BF16), wy_ref[...])
    for t in range(S5_CHUNK):
        y_ref[:, t, :] = y[:, t * 128:(t + 1) * 128]


def _s5(u3, x0, mats, nc):
    rows_all = u3.shape[0]
    bsz = rows_all // nc
    rows = min(S5_ROWS_PER_STEP, rows_all)
    width = S5_CHUNK * 128
    state = 2 * S5_BLOCK_STATE
    const = lambda shape: pl.BlockSpec((None,) + shape, lambda j, r: (j, 0, 0))
    tile = pl.BlockSpec((rows, S5_CHUNK, 128), lambda j, r: (r, 0, j))
    return pl.pallas_call(
        functools.partial(_s5_kernel, nc=nc),
        grid=(S5_LANE_BLOCKS, rows_all // rows),
        in_specs=[tile, const((bsz, state)), const((width, state)), const((state, width)),
                  const((width, width)), const((1, state))],
        out_specs=[tile, const((bsz, state))],
        out_shape=[jax.ShapeDtypeStruct(u3.shape, F32),
                   jax.ShapeDtypeStruct((S5_LANE_BLOCKS, bsz, state), F32)],
        scratch_shapes=[pltpu.VMEM((rows, state), F32), pltpu.VMEM((8, S5_BLOCK_STATE), F32)],
        compiler_params=_params(("parallel", "arbitrary")),
        name="s5",
    )(u3, x0, mats["wst"], mats["wy"], mats["tmat"], mats["dec"])


def _s5_matrices(a_re, a_im, log_dt, b_re, b_im, c_re, c_im, d):
    L = S5_CHUNK
    dt = jnp.exp(log_dt)[:, None]
    n = jnp.arange(L + 1, dtype=F32)[:, None, None]
    mag = jnp.exp(a_re * dt * n)
    ang = a_im * dt * n
    pw_re, pw_im = mag * jnp.cos(ang), mag * jnp.sin(ang)
    num_re, num_im = pw_re[1] - 1.0, pw_im[1]
    den = a_re * a_re + a_im * a_im
    f_re = (num_re * a_re + num_im * a_im) / den
    f_im = (num_im * a_re - num_re * a_im) / den
    bb_re = f_re[..., None] * b_re - f_im[..., None] * b_im
    bb_im = f_re[..., None] * b_im + f_im[..., None] * b_re
    pb_re = pw_re[..., None] * bb_re - pw_im[..., None] * bb_im
    pb_im = pw_re[..., None] * bb_im + pw_im[..., None] * bb_re
    kern = (jnp.einsum('gcp,ngpd->ngcd', c_re, pb_re[:L]) - jnp.einsum('gcp,ngpd->ngcd', c_im, pb_im[:L]))
    s_idx = jnp.arange(L)[:, None]
    t_idx = jnp.arange(L)[None, :]
    tk = kern[jnp.clip(t_idx - s_idx, 0, L - 1)]
    tk = jnp.where((t_idx >= s_idx)[:, :, None, None, None], tk, 0.0)
    skip = (jnp.eye(L, dtype=F32)[:, :, None, None, None] * d.reshape(1, 1, S5_GROUPS, S5_CH, 1)
            * jnp.eye(S5_CH, dtype=F32)[None, None, None])
    tk = tk + skip
    nb, bg = S5_LANE_BLOCKS, S5_BLOCK_GROUPS

    def block_diag(compact, row_inner, outer, inner):
        src = jnp.arange(outer * inner)
        dst = jnp.arange(outer * bg * inner)
        pick = ((src[:, None] // inner == dst[None, :] // (bg * inner))
                & (src[:, None] % inner == dst[None, :] % inner)).astype(BF16)
        wide = jnp.einsum('jrk,kn->jrn', compact.astype(BF16), pick, preferred_element_type=F32)
        row_group = (jnp.arange(compact.shape[1]) // row_inner) % bg
        keep = row_group[:, None] == (dst[None, :] // inner) % bg
        return jnp.where(keep, wide, 0.0).astype(BF16)

    tk = tk.reshape(L, L, nb, bg, S5_CH, S5_CH).transpose(2, 0, 3, 5, 1, 4)
    tmat = block_diag(tk.reshape(nb, L * 128, L * S5_CH), S5_CH, L, S5_CH)
    inj = jnp.stack([pb_re[:L][::-1], pb_im[:L][::-1]], axis=0)
    inj = inj.reshape(2, L, nb, bg, S5_STATE, S5_CH).transpose(2, 1, 3, 5, 0, 4)
    wst = block_diag(inj.reshape(nb, L * 128, 2 * S5_STATE), S5_CH, 2, S5_STATE)
    cp_re = c_re[None] * pw_re[1:, :, None, :] - c_im[None] * pw_im[1:, :, None, :]
    cp_im = c_re[None] * pw_im[1:, :, None, :] + c_im[None] * pw_re[1:, :, None, :]
    rd = jnp.stack([cp_re, -cp_im], axis=0).reshape(2, L, nb, bg, S5_CH, S5_STATE)
    rd = rd.transpose(2, 0, 3, 5, 1, 4)
    wy = block_diag(rd.reshape(nb, 2 * bg * S5_STATE, L * S5_CH), S5_STATE, L, S5_CH)
    dec = jnp.concatenate([pw_re[L].reshape(nb, 1, bg * S5_STATE), pw_im[L].reshape(nb, 1, bg * S5_STATE)],
                          axis=2)
    return {"wst": wst.astype(BF16), "wy": wy.astype(BF16), "tmat": tmat.astype(BF16), "dec": dec}


def _mix_out_kernel(o_ref, bonus_ref, g_ref, y_ref, x_ref, gnw_ref, gnb_ref, wglu_ref, bglu_ref,
                    wout_ref, ones_ref, x1_ref):
    ones_bd = ones_ref[...]
    o = o_ref[...]
    mu = _head_sum(o, ones_bd) * (1.0 / HEAD_DIM)
    d = o - mu
    var = _head_sum(d * d, ones_bd) * (1.0 / HEAD_DIM)
    o = d * lax.rsqrt(var + GN_EPS) * gnw_ref[...] + gnb_ref[...]
    o = (o + bonus_ref[...]) * g_ref[...]
    y = _gelu(y_ref[...])
    y = y * _sigmoid(_dot(y.astype(BF16), wglu_ref[...]) + bglu_ref[...])
    mixed = jnp.concatenate([o, y], axis=1).astype(BF16)
    x1_ref[...] = x_ref[...] + _dot(mixed, wout_ref[...])


def _mix_out(o, bonus, g, y, x2d, prm, tm):
    n = x2d.shape[0]
    tile = pl.BlockSpec((tm, RWKV_WIDTH), lambda i: (i, 0))
    row = pl.BlockSpec((1, RWKV_WIDTH), lambda i: (0, 0))
    sq = pl.BlockSpec((RWKV_WIDTH, RWKV_WIDTH), lambda i: (0, 0))
    return pl.pallas_call(
        _mix_out_kernel,
        grid=(n // tm,),
        in_specs=[tile, tile, tile, tile, pl.BlockSpec((tm, D_MODEL), lambda i: (i, 0)),
                  row, row, sq, row, pl.BlockSpec((D_MODEL, D_MODEL), lambda i: (0, 0)), sq],
        out_specs=pl.BlockSpec((tm, D_MODEL), lambda i: (i, 0)),
        out_shape=jax.ShapeDtypeStruct((n, D_MODEL), F32),
        compiler_params=_params(("parallel",)),
        name="mix_out",
    )(o, bonus, g, y, x2d, prm["gn_w"], prm["gn_b"], prm["w_glu"], prm["b_glu"], prm["w_out"],
      prm["ones_bd"])


def _sort16_pairs():
    def merge(lo, hi, r):
        step = 2 * r
        if step < hi - lo:
            yield from merge(lo, hi, step)
            yield from merge(lo + r, hi, step)
            yield from ((i, i + r) for i in range(lo + r, hi - r, step))
        else:
            yield (lo, lo + r)

    def sort(lo, hi):
        if hi > lo:
            mid = lo + (hi - lo) // 2
            yield from sort(lo, mid)
            yield from sort(mid + 1, hi)
            yield from merge(lo, hi, 1)

    return list(sort(0, PEER_TOPK - 1))


def _top16(x):
    sub = x.shape[0] // PEER_TOPK
    v = [x[k * sub:(k + 1) * sub, :] for k in range(PEER_TOPK)]
    for i, j in _sort16_pairs():
        v[i], v[j] = jnp.maximum(v[i], v[j]), jnp.minimum(v[i], v[j])
    vals = []
    for q in range(PEER_TOPK):
        m = jnp.max(v[0], axis=0, keepdims=True)
        vals.append(m)
        if q + 1 < PEER_TOPK:
            hit = v[0] == m
            for k in range(PEER_TOPK - 1 - q):
                v[k] = jnp.where(hit, v[k + 1], v[k])
    return vals


def _peer_route_kernel(x_ref, g_ref, wqt_ref, k1_ref, k2_ref,
                       h2t_ref, e1_ref, cnt_ref, rank2_ref, e2_ref):
    h2 = _rmsnorm(x_ref[...], g_ref[...])
    h2_b = h2.astype(BF16)
    h2t_ref[...] = jnp.transpose(h2).astype(BF16)
    qt = _dot_nt(wqt_ref[...], h2_b)
    for h in range(PEER_HEADS):
        q1 = qt[h * PEER_QDIM:h * PEER_QDIM + PEER_HALF].astype(BF16)
        q2 = qt[h * PEER_QDIM + PEER_HALF:(h + 1) * PEER_QDIM].astype(BF16)
        s1 = _dot(k1_ref[h], q1)
        s2 = _dot(k2_ref[h], q2)
        a1 = _top16(s1)
        a2 = _top16(s2)
        a1_mat = jnp.concatenate(a1, axis=0)
        a2_lo = jnp.concatenate(a2[:8], axis=0)
        cand = [a1[0] + a2_lo, a1[0] + jnp.concatenate(a2[8:], axis=0)]
        cand += [a1[i] + a2_lo for i in range(1, 8)]
        cand.append(a1_mat[8:] + a2[0])
        cand = jnp.concatenate(cand, axis=0)
        top = a1[0] + a2[0]
        zsum = jnp.zeros_like(top)
        thr = top
        for _ in range(PEER_TOPK):
            thr = jnp.max(cand, axis=0, keepdims=True)
            zsum = zsum + jnp.exp(thr - top)
            cand = jnp.where(cand == thr, -jnp.inf, cand)
        cnt = jnp.zeros_like(s1)
        for q in range(PEER_TOPK):
            reach = (a1_mat + a2[q]) >= thr
            b_q = jnp.min(jnp.where(reach, a1_mat, jnp.inf), axis=0, keepdims=True)
            cnt = jnp.where(s1 >= b_q, float(q + 1), cnt)
        rank2 = jnp.full(s2.shape, float(PEER_TOPK), F32)
        for q in reversed(range(PEER_TOPK)):
            rank2 = jnp.where(s2 >= a2[q], float(q), rank2)
        e1_ref[h] = jnp.exp(s1 - a1[0])
        cnt_ref[h] = cnt
        rank2_ref[h] = rank2.astype(BF16)
        e2_ref[h] = (jnp.exp(s2 - a2[0]) / zsum).astype(BF16)


def _peer_route(x1, ln2_g, wqt_bf, k1_bf, k2_bf, tt):
    n = x1.shape[0]
    keys = pl.BlockSpec((PEER_HEADS, N_KEYS, PEER_HALF), lambda i: (0, 0, 0))
    tok = pl.BlockSpec((PEER_HEADS, N_KEYS, tt), lambda i: (0, 0, i))
    per_key = lambda dt: jax.ShapeDtypeStruct((PEER_HEADS, N_KEYS, n), dt)
    return pl.pallas_call(
        _peer_route_kernel,
        grid=(n // tt,),
        in_specs=[
            pl.BlockSpec((tt, D_MODEL), lambda i: (i, 0)),
            pl.BlockSpec((1, D_MODEL), lambda i: (0, 0)),
            pl.BlockSpec((PEER_HEADS * PEER_QDIM, D_MODEL), lambda i: (0, 0)),
            keys, keys,
        ],
        out_specs=[pl.BlockSpec((D_MODEL, tt), lambda i: (0, i)), tok, tok, tok, tok],
        out_shape=[jax.ShapeDtypeStruct((D_MODEL, n), BF16), per_key(F32), per_key(F32), per_key(BF16),
                   per_key(BF16)],
        compiler_params=_params(("parallel",)),
        name="peer_route",
    )(x1, ln2_g, wqt_bf, k1_bf, k2_bf)


PEER_CHUNK = 1024
PEER_CHUNK_KEYS = PEER_CHUNK // N_KEYS
PEER_PIECE_ROWS = 16
PEER_PIECE_LANES = 128


def _peer_dense_kernel(h2t_ref, u_ref, vt_ref, e1_ref, cnt_ref, rank2_in_ref, e2_in_ref, x1_ref, g_ref,
                       y_ref, acc_scr, a0_scr, a1_scr, w0_scr, w1_scr, tab_scr):
    c = pl.program_id(1)
    tt = h2t_ref.shape[1]

    @pl.when(c == 0)
    def _():
        acc_scr[...] = jnp.zeros_like(acc_scr)
        for sub in range(N_KEYS // PEER_PIECE_ROWS):
            second = slice(sub * PEER_PIECE_ROWS, (sub + 1) * PEER_PIECE_ROWS)
            for blk in range(tt // PEER_PIECE_LANES):
                lanes = slice(blk * PEER_PIECE_LANES, (blk + 1) * PEER_PIECE_LANES)
                for h in range(PEER_HEADS):
                    tab_scr[sub, blk, h, 0] = rank2_in_ref[h, second, lanes]
                    tab_scr[sub, blk, h, 1] = e2_in_ref[h, second, lanes]

    def activations(key, a_scr):
        a_scr[...] = _dot(u_ref[key], h2t_ref[...])

    def accumulate(key, w_scr):
        acc_scr[...] += _dot(vt_ref[key], w_scr[...])

    def build(key, a_scr, w_scr):
        cnt_rows = [cnt_ref[h, pl.ds(key, 1), :] for h in range(PEER_HEADS)]
        e1_rows = [e1_ref[h, pl.ds(key, 1), :] for h in range(PEER_HEADS)]
        for blk in range(tt // PEER_PIECE_LANES):
            lanes = slice(blk * PEER_PIECE_LANES, (blk + 1) * PEER_PIECE_LANES)
            tile = (PEER_PIECE_ROWS, PEER_PIECE_LANES)
            cnt_b = [jnp.broadcast_to(cnt_rows[h][:, lanes], tile).astype(BF16) for h in range(PEER_HEADS)]
            e1_b = [jnp.broadcast_to(e1_rows[h][:, lanes], tile).astype(BF16) for h in range(PEER_HEADS)]
            for sub in range(N_KEYS // PEER_PIECE_ROWS):
                gate = None
                for h in range(PEER_HEADS):
                    sel = jnp.where(tab_scr[sub, blk, h, 0] < cnt_b[h], tab_scr[sub, blk, h, 1],
                                    jnp.zeros(tile, BF16))
                    term = sel * e1_b[h]
                    gate = term if gate is None else gate + term
                rows = slice(sub * PEER_PIECE_ROWS, (sub + 1) * PEER_PIECE_ROWS)
                w_scr[rows, lanes] = _gelu(a_scr[rows, lanes]).astype(BF16) * gate

    activations(0, a0_scr)
    w1_scr[...] = jnp.zeros_like(w1_scr)

    def body(it, carry):
        even = 2 * it
        odd = even + 1
        activations(odd, a1_scr)
        accumulate(jnp.maximum(even - 1, 0), w1_scr)
        build(even, a0_scr, w0_scr)
        activations(jnp.minimum(odd + 1, PEER_CHUNK_KEYS - 1), a0_scr)
        accumulate(even, w0_scr)
        build(odd, a1_scr, w1_scr)
        return carry

    lax.fori_loop(0, PEER_CHUNK_KEYS // 2, body, 0)
    accumulate(PEER_CHUNK_KEYS - 1, w1_scr)

    @pl.when(c == pl.num_programs(1) - 1)
    def _():
        x2 = x1_ref[...] + jnp.transpose(acc_scr[...])
        y_ref[...] = _rmsnorm(x2, g_ref[...])


def _peer_dense(h2t, u_bf, vt_bf, e1, cnt, rank2, e2, x1, lnf_g, tt):
    n = x1.shape[0]
    nchunks = N_EXPERTS // PEER_CHUNK
    by_chunk = lambda z: z.reshape(PEER_HEADS, nchunks, PEER_CHUNK_KEYS, n)
    tok_all = pl.BlockSpec((PEER_HEADS, N_KEYS, tt), lambda i, c: (0, 0, i))
    tok_chunk = pl.BlockSpec((PEER_HEADS, None, PEER_CHUNK_KEYS, tt), lambda i, c: (0, c, 0, i))
    return pl.pallas_call(
        _peer_dense_kernel,
        grid=(n // tt, nchunks),
        in_specs=[
            pl.BlockSpec((D_MODEL, tt), lambda i, c: (0, i)),
            pl.BlockSpec((PEER_CHUNK_KEYS, N_KEYS, D_MODEL), lambda i, c: (c, 0, 0)),
            pl.BlockSpec((PEER_CHUNK_KEYS, D_MODEL, N_KEYS), lambda i, c: (c, 0, 0)),
            tok_chunk, tok_chunk, tok_all, tok_all,
            pl.BlockSpec((tt, D_MODEL), lambda i, c: (i, 0)),
            pl.BlockSpec((1, D_MODEL), lambda i, c: (0, 0)),
        ],
        out_specs=pl.BlockSpec((tt, D_MODEL), lambda i, c: (i, 0)),
        out_shape=jax.ShapeDtypeStruct((n, D_MODEL), F32),
        scratch_shapes=[pltpu.VMEM((D_MODEL, tt), F32),
                        pltpu.VMEM((N_KEYS, tt), F32), pltpu.VMEM((N_KEYS, tt), F32),
                        pltpu.VMEM((N_KEYS, tt), BF16), pltpu.VMEM((N_KEYS, tt), BF16),
                        pltpu.VMEM((N_KEYS // PEER_PIECE_ROWS, tt // PEER_PIECE_LANES, PEER_HEADS, 2,
                                    PEER_PIECE_ROWS, PEER_PIECE_LANES), BF16)],
        compiler_params=_params(("parallel", "arbitrary")),
        name="peer_dense",
    )(h2t, u_bf, vt_bf, by_chunk(e1), by_chunk(cnt), rank2, e2, x1, lnf_g)


def _pick_tile(n, target):
    t = min(n, target)
    while n % t:
        t //= 2
    return t


def _layer(x, wkv0, shift0, s5re0, s5im0, prm, s5m, lnf_g):
    bsz, t_len, _ = x.shape
    n = bsz * t_len
    x2d = x.reshape(n, D_MODEL)
    tm = _pick_tile(t_len, 512)
    prw, u = _in_proj(x2d, prm["ln1_g"], prm["w_in"], tm)

    tiles_per_seq = t_len // tm
    prw3 = prw.reshape(bsz, t_len, RWKV_PROJ)
    last_rows = prw3[:, tm - 1::tm, :]
    bound = jnp.concatenate([shift0, last_rows[:, :tiles_per_seq - 1, :]], axis=1)
    bound = bound.reshape(bsz * tiles_per_seq, 1, RWKV_PROJ)
    r, lw, k, v, kk, kka, g, bonus = _rwkv_prep(prw, bound, prm, tm)

    t_pad = -(-t_len // WKV_CHUNK) * WKV_CHUNK
    seq3 = lambda z: jnp.pad(z.reshape(bsz, t_len, RWKV_WIDTH), ((0, 0), (0, t_pad - t_len), (0, 0)))
    st = wkv0.reshape(bsz, HEAD_PAIRS, 2, HEAD_DIM, HEAD_DIM).transpose(0, 1, 2, 4, 3)
    eye2 = jnp.eye(2, dtype=F32)
    s0_bd = jnp.einsum('bphkv,hg->bphkgv', st, eye2).reshape(bsz, HEAD_PAIRS, PAIR_LANES, PAIR_LANES)
    o, s1_bd = _wkv(seq3(r), seq3(lw), seq3(k), seq3(v), seq3(kk), seq3(kka), s0_bd)
    o = o[:, :t_len].reshape(n, RWKV_WIDTH)
    s1 = s1_bd.reshape(bsz, HEAD_PAIRS, 2, HEAD_DIM, 2, HEAD_DIM)
    s1 = jnp.stack([s1[:, :, 0, :, 0, :], s1[:, :, 1, :, 1, :]], axis=2)
    wkv1 = s1.transpose(0, 1, 2, 4, 3).reshape(bsz, RWKV_HEADS, HEAD_DIM, HEAD_DIM)

    nc = t_len // S5_CHUNK
    blk = lambda z: z.reshape(bsz, S5_LANE_BLOCKS, S5_BLOCK_STATE).transpose(1, 0, 2)
    x0 = jnp.concatenate([blk(s5re0), blk(s5im0)], axis=2)
    y3, x1s = _s5(u.reshape(bsz * nc, S5_CHUNK, S5_WIDTH), x0, s5m, nc)
    y = y3.reshape(n, S5_WIDTH)
    unblk = lambda z: z.transpose(1, 0, 2).reshape(bsz, S5_GROUPS, S5_STATE)
    s5re1, s5im1 = unblk(x1s[:, :, :S5_BLOCK_STATE]), unblk(x1s[:, :, S5_BLOCK_STATE:])

    x1 = _mix_out(o, bonus, g, y, x2d, prm, tm)

    tt = _pick_tile(n, 512)
    h2t, e1, cnt, rank2, e2 = _peer_route(x1, prm["ln2_g"], prm["w_qt"], prm["keys1"], prm["keys2"], tt)
    yout = _peer_dense(h2t, prm["peer_u"], prm["peer_vt"], e1, cnt, rank2, e2, x1, lnf_g, tt)
    shift1 = prw3[:, -1:, :]
    return yout.reshape(bsz, t_len, D_MODEL), wkv1, shift1, s5re1, s5im1


def kernel(x_prompt, x_sample, state_wkv, state_shift, state_s5_re, state_s5_im, ln1_g, w_in, rwkv_mu, rwkv_w0, rwkv_w2, rwkv_a0, rwkv_a2, rwkv_g2, rwkv_k_k, rwkv_k_a, rwkv_r_k, rwkv_gn_w, rwkv_gn_b, s5_a_re, s5_a_im, s5_log_dt, s5_b_re, s5_b_im, s5_c_re, s5_c_im, s5_d, s5_w_glu, s5_b_glu, w_out, ln2_g, peer_w_q, peer_keys1, peer_keys2, peer_u, peer_v, lnf_g):
    depth = w_in.shape[0]
    assert depth == 1
    l = 0
    row = lambda z: z.reshape(1, -1).astype(F32)
    lora = jnp.zeros((LORA_WIDTH, 3 * RWKV_WIDTH), F32)
    lora = lora.at[:64, :RWKV_WIDTH].set(rwkv_w2[l])
    lora = lora.at[64:128, RWKV_WIDTH:2 * RWKV_WIDTH].set(rwkv_a2[l])
    lora = lora.at[128:, 2 * RWKV_WIDTH:].set(rwkv_g2[l])
    lane = jnp.arange(RWKV_WIDTH)
    ones_bd = (lane[:, None] // HEAD_DIM == lane[None, :] // HEAD_DIM).astype(BF16)
    prm = {
        "ln1_g": row(ln1_g[l]), "w_in": w_in[l].astype(BF16), "mu": row(rwkv_mu[l]),
        "w0": row(rwkv_w0[l]), "a0": row(rwkv_a0[l]), "lora": lora.astype(BF16),
        "k_k": row(rwkv_k_k[l]), "k_a": row(rwkv_k_a[l]), "r_k": row(rwkv_r_k[l]),
        "gn_w": row(rwkv_gn_w[l]), "gn_b": row(rwkv_gn_b[l]), "ones_bd": ones_bd,
        "w_glu": s5_w_glu[l].astype(BF16), "b_glu": row(s5_b_glu[l]), "w_out": w_out[l].astype(BF16),
        "ln2_g": row(ln2_g[l]), "w_qt": peer_w_q[l].T.astype(BF16),
        "keys1": peer_keys1[l].astype(BF16), "keys2": peer_keys2[l].astype(BF16),
        "peer_u": peer_u[l].astype(BF16).reshape(N_KEYS, N_KEYS, D_MODEL),
        "peer_vt": peer_v[l].astype(BF16).reshape(N_KEYS, N_KEYS, D_MODEL).transpose(0, 2, 1),
    }
    s5m = _s5_matrices(s5_a_re[l], s5_a_im[l], s5_log_dt[l], s5_b_re[l], s5_b_im[l], s5_c_re[l],
                       s5_c_im[l], s5_d[l])
    lnf = row(lnf_g)
    bp = x_prompt.shape[0]
    zeros = lambda *s: jnp.zeros(s, F32)
    yp, a1, a2, a3, a4 = _layer(
        x_prompt, zeros(bp, RWKV_HEADS, HEAD_DIM, HEAD_DIM), zeros(bp, 1, RWKV_PROJ),
        zeros(bp, S5_GROUPS, S5_STATE), zeros(bp, S5_GROUPS, S5_STATE), prm, s5m, lnf)
    ys, b1, b2, b3, b4 = _layer(x_sample, state_wkv[l], state_shift[l], state_s5_re[l], state_s5_im[l],
                                prm, s5m, lnf)
    st = lambda z: z[None]
    return (yp, ys, st(a1), st(a2), st(a3), st(a4), st(b1), st(b2), st(b3), st(b4))
```

```python
import functools
import math

import jax
import jax.numpy as jnp
from jax import lax
from jax.experimental import pallas as pl
from jax.experimental.pallas import tpu as pltpu

F32 = jnp.float32
BF16 = jnp.bfloat16

D_MODEL = 1024
RWKV_WIDTH = 512
HEAD_DIM = 64
RWKV_HEADS = 8
HEAD_PAIRS = RWKV_HEADS // 2
PAIR_LANES = 2 * HEAD_DIM
LORA_WIDTH = 256
RWKV_PROJ = 3 * RWKV_WIDTH + LORA_WIDTH
S5_WIDTH = 512
S5_CH = 16
S5_GROUPS = 32
S5_STATE = 64
S5_CHUNK = 16
IN_PROJ = RWKV_PROJ + S5_WIDTH
PEER_HEADS = 8
N_KEYS = 128
N_EXPERTS = N_KEYS * N_KEYS
PEER_TOPK = 16
PEER_QDIM = 256
PEER_HALF = 128
NORM_EPS = 1e-6
GN_EPS = HEAD_DIM * 1e-5
WKV_CHUNK = 64

VMEM_LIMIT_BYTES = 48 * 1024 * 1024


def _params(semantics, flags=None):
    return pltpu.CompilerParams(dimension_semantics=semantics, vmem_limit_bytes=VMEM_LIMIT_BYTES,
                                flags=flags)


def _dot(a, b):
    return jnp.dot(a, b, preferred_element_type=F32)


def _dot_nt(a, b):
    return lax.dot_general(a, b, (((1,), (1,)), ((), ())), preferred_element_type=F32)


def _dot_tn(a, b):
    return lax.dot_general(a, b, (((0,), (0,)), ((), ())), preferred_element_type=F32)


def _split2(x):
    hi = x.astype(BF16)
    lo = (x - hi.astype(F32)).astype(BF16)
    return hi, lo


def _split3(x):
    hi = x.astype(BF16)
    r1 = x - hi.astype(F32)
    mid = r1.astype(BF16)
    lo = (r1 - mid.astype(F32)).astype(BF16)
    return hi, mid, lo


def _head_sum(x, ones_bd):
    hi, lo = _split2(x)
    return _dot(hi, ones_bd) + _dot(lo, ones_bd)


def _rmsnorm(x, g):
    return x * lax.rsqrt(jnp.mean(x * x, axis=-1, keepdims=True) + NORM_EPS) * g


def _gelu(x):
    c = math.sqrt(2.0 / math.pi)
    half = 0.5 * x
    return half * jnp.tanh(x * (c + (c * 0.044715) * (x * x))) + half


def _sigmoid(x):
    return 1.0 / (1.0 + jnp.exp(-x))


def _in_proj_kernel(x_ref, g_ref, w_ref, prw_ref, u_ref):
    h = _rmsnorm(x_ref[...], g_ref[...]).astype(BF16)
    p = _dot(h, w_ref[...])
    prw_ref[...] = p[:, :RWKV_PROJ]
    u_ref[...] = p[:, RWKV_PROJ:]


def _in_proj(x2d, ln1_g, w_in_bf, tm):
    n = x2d.shape[0]
    return pl.pallas_call(
        _in_proj_kernel,
        grid=(n // tm,),
        in_specs=[
            pl.BlockSpec((tm, D_MODEL), lambda i: (i, 0)),
            pl.BlockSpec((1, D_MODEL), lambda i: (0, 0)),
            pl.BlockSpec((D_MODEL, IN_PROJ), lambda i: (0, 0)),
        ],
        out_specs=[
            pl.BlockSpec((tm, RWKV_PROJ), lambda i: (i, 0)),
            pl.BlockSpec((tm, S5_WIDTH), lambda i: (i, 0)),
        ],
        out_shape=[
            jax.ShapeDtypeStruct((n, RWKV_PROJ), F32),
            jax.ShapeDtypeStruct((n, S5_WIDTH), F32),
        ],
        compiler_params=_params(("parallel",)),
        name="in_proj",
    )(x2d, ln1_g, w_in_bf)


def _rwkv_prep_kernel(p_ref, bound_ref, mu_ref, w0_ref, a0_ref, lora_ref, kk_ref_, ka_ref, rk_ref,
                      ones_ref, r_out, lw_out, k_out, v_out, kk_out, kka_out, g_out, bonus_out):
    p = p_ref[...]
    tm = p.shape[0]
    row = lax.broadcasted_iota(jnp.int32, p.shape, 0)
    p_prev = jnp.where(row == 0, bound_ref[0], pltpu.roll(p, 1, axis=0))
    m = p + (p_prev - p) * mu_ref[...]
    r = m[:, :RWKV_WIDTH]
    k = m[:, RWKV_WIDTH:2 * RWKV_WIDTH]
    v = m[:, 2 * RWKV_WIDTH:3 * RWKV_WIDTH]
    z = m[:, 3 * RWKV_WIDTH:]
    lane = lax.broadcasted_iota(jnp.int32, (tm, LORA_WIDTH), 1)
    feat = jnp.where(lane < 64, jnp.tanh(z), jnp.where(lane < 128, z, _sigmoid(z)))
    lo = _dot(feat.astype(BF16), lora_ref[...])
    wl = -(w0_ref[...] + lo[:, :RWKV_WIDTH])
    softplus = jnp.maximum(wl, 0.0) + jnp.log(1.0 + jnp.exp(-jnp.abs(wl)))
    lw_out[...] = -jnp.exp(-softplus - 0.5)
    a = _sigmoid(a0_ref[...] + lo[:, RWKV_WIDTH:2 * RWKV_WIDTH])
    g_out[...] = lo[:, 2 * RWKV_WIDTH:]
    ones_bd = ones_ref[...]
    kk = k * kk_ref_[...]
    nrm = jnp.sqrt(_head_sum(kk * kk, ones_bd))
    kk = kk / jnp.maximum(nrm, 1e-12)
    k2 = k * (1.0 + (a - 1.0) * ka_ref[...])
    r_out[...] = r
    k_out[...] = k2
    v_out[...] = v
    kk_out[...] = kk
    kka_out[...] = kk * a
    bonus_out[...] = _head_sum(r * k2 * rk_ref[...], ones_bd) * v


def _rwkv_prep(prw, bound, prm, tm):
    n = prw.shape[0]
    row = lambda width: pl.BlockSpec((1, width), lambda i: (0, 0))
    tile = pl.BlockSpec((tm, RWKV_WIDTH), lambda i: (i, 0))
    return pl.pallas_call(
        _rwkv_prep_kernel,
        grid=(n // tm,),
        in_specs=[
            pl.BlockSpec((tm, RWKV_PROJ), lambda i: (i, 0)),
            pl.BlockSpec((1, 1, RWKV_PROJ), lambda i: (i, 0, 0)),
            row(RWKV_PROJ), row(RWKV_WIDTH), row(RWKV_WIDTH),
            pl.BlockSpec((LORA_WIDTH, 3 * RWKV_WIDTH), lambda i: (0, 0)),
            row(RWKV_WIDTH), row(RWKV_WIDTH), row(RWKV_WIDTH),
            pl.BlockSpec((RWKV_WIDTH, RWKV_WIDTH), lambda i: (0, 0)),
        ],
        out_specs=[tile] * 8,
        out_shape=[jax.ShapeDtypeStruct((n, RWKV_WIDTH), F32)] * 8,
        compiler_params=_params(("parallel",)),
        name="rwkv_prep",
    )(prw, bound, prm["mu"], prm["w0"], prm["a0"], prm["lora"], prm["k_k"], prm["k_a"], prm["r_k"],
      prm["ones_bd"])


def _wkv_units(units, s_list, tri, bd_strict, bd_incl, eye2, eye_s, lane_lo):
    seq = units[0][0].shape[0]
    two = 2 * seq
    n = len(units)
    rng = range(n)

    def stack(z):
        return jnp.concatenate([jnp.where(lane_lo, z, 0.0), jnp.where(lane_lo, 0.0, z)], axis=0)

    cl = []
    for (r, lw, k, v, kk, kka) in units:
        hi, mid, lo = _split3(lw)
        cl.append(_dot(tri, hi) + _dot(tri, mid) + _dot(tri, lo))
    gam = [jnp.exp(c) for c in cl]
    gam_inv = [jnp.exp(-c) for c in cl]
    gam_prev = [jnp.exp(cl[i] - units[i][1]) for i in rng]
    gam_last = [g[seq - 1:seq, :] for g in gam]
    a_t = [-units[i][4] * gam_prev[i] for i in rng]
    r_t = [units[i][0] * gam[i] for i in rng]
    b_t = [units[i][5] * gam_inv[i] for i in rng]
    k_t = [units[i][2] * gam_inv[i] for i in rng]
    a_st = [stack(z) for z in a_t]
    r_st = [stack(z) for z in r_t]
    v_st_b = [stack(u[3]).astype(BF16) for u in units]
    bh_st = [stack(b_t[i] * gam_last[i]).astype(BF16) for i in rng]
    kh_st = [stack(k_t[i] * gam_last[i]).astype(BF16) for i in rng]
    x = [_dot_nt(jnp.concatenate([a_st[i], r_st[i]], axis=0).astype(BF16),
                 jnp.concatenate([b_t[i], b_t[i], k_t[i], k_t[i]], axis=0).astype(BF16)) for i in rng]
    m_ab = [jnp.where(bd_strict, z[:two, :two], 0.0) for z in x]
    m_akv = [_dot(jnp.where(bd_strict, x[i][:two, two:], 0.0).astype(BF16), v_st_b[i]) for i in rng]
    n_cat = [jnp.concatenate([jnp.where(bd_incl, z[two:, :two], 0.0),
                              jnp.where(bd_incl, z[two:, two:], 0.0)], axis=1).astype(BF16) for z in x]
    t_inv = [eye2 + m for m in m_ab]
    pw = m_ab
    steps = 1
    while steps * 2 < seq:
        pw_b = [p.astype(BF16) for p in pw]
        pw = [_dot(p, p) for p in pw_b]
        t_inv = [_dot(t_inv[i].astype(BF16), (eye2 + pw[i]).astype(BF16)) for i in rng]
        steps *= 2
    ty = [_dot(t_inv[i].astype(BF16), jnp.concatenate([a_st[i], m_akv[i]], axis=1).astype(BF16))
          for i in rng]
    ah_b = [z[:, :PAIR_LANES].astype(BF16) for z in ty]
    uh_b = [z[:, PAIR_LANES:].astype(BF16) for z in ty]
    p_t = [_dot_tn(bh_st[i], ah_b[i]) + jnp.where(eye_s, gam_last[i], 0.0) for i in rng]
    q_t = [_dot_tn(jnp.concatenate([bh_st[i], kh_st[i]], axis=0),
                   jnp.concatenate([uh_b[i], v_st_b[i]], axis=0)) for i in rng]
    ro = [_dot(n_cat[i], jnp.concatenate(
        [jnp.concatenate([ah_b[i], uh_b[i]], axis=1),
         jnp.concatenate([jnp.zeros_like(v_st_b[i]), v_st_b[i]], axis=1)], axis=0)) for i in rng]
    outs, states = [], []
    for i in rng:
        s_hi, s_lo = _split2(s_list[i])
        rh_hi, rh_lo = _split2(r_st[i] + ro[i][:, :PAIR_LANES])
        o_st = _dot(rh_hi, s_hi) + _dot(rh_hi, s_lo) + _dot(rh_lo, s_hi) + ro[i][:, PAIR_LANES:]
        outs.append(o_st[:seq] + o_st[seq:])
        p_hi, p_lo = _split2(p_t[i])
        states.append(_dot(p_hi, s_hi) + _dot(p_hi, s_lo) + _dot(p_lo, s_hi) + q_t[i])
    return outs, states


def _wkv_kernel(r_ref, lw_ref, k_ref, v_ref, kk_ref, kka_ref, s0_ref, o_ref, s1_ref, s_scr):
    j = pl.program_id(1)
    nb, seq = r_ref.shape[0], r_ref.shape[1]
    two = 2 * seq

    @pl.when(j == 0)
    def _():
        s_scr[...] = s0_ref[...]

    ri = lax.broadcasted_iota(jnp.int32, (seq, seq), 0)
    ci = lax.broadcasted_iota(jnp.int32, (seq, seq), 1)
    tri = (ri >= ci).astype(BF16)
    r2 = lax.broadcasted_iota(jnp.int32, (two, two), 0)
    c2 = lax.broadcasted_iota(jnp.int32, (two, two), 1)
    same = (r2 // seq) == (c2 // seq)
    bd_strict = same & ((r2 % seq) > (c2 % seq))
    bd_incl = same & ((r2 % seq) >= (c2 % seq))
    eye2 = (r2 == c2).astype(F32)
    rs = lax.broadcasted_iota(jnp.int32, (PAIR_LANES, PAIR_LANES), 0)
    cs = lax.broadcasted_iota(jnp.int32, (PAIR_LANES, PAIR_LANES), 1)
    eye_s = rs == cs
    lane_lo = lax.broadcasted_iota(jnp.int32, (seq, PAIR_LANES), 1) < HEAD_DIM

    ids = [(b, pair) for b in range(nb) for pair in range(HEAD_PAIRS)]
    lanes = lambda pair: slice(pair * PAIR_LANES, (pair + 1) * PAIR_LANES)
    units = [tuple(ref[b, :, lanes(pair)] for ref in (r_ref, lw_ref, k_ref, v_ref, kk_ref, kka_ref))
             for (b, pair) in ids]
    s_list = [s_scr[b, pair] for (b, pair) in ids]
    outs, states = _wkv_units(units, s_list, tri, bd_strict, bd_incl, eye2, eye_s, lane_lo)
    for (b, pair), o, s_new in zip(ids, outs, states):
        o_ref[b, :, lanes(pair)] = o
        s_scr[b, pair] = s_new

    @pl.when(j == pl.num_programs(1) - 1)
    def _():
        s1_ref[...] = s_scr[...]


WKV_BATCH_PER_STEP = 2


def _wkv(r, lw, k, v, kk, kka, s0_bd):
    bsz, t_len, _ = r.shape
    nc = t_len // WKV_CHUNK
    nb = WKV_BATCH_PER_STEP
    tile = pl.BlockSpec((nb, WKV_CHUNK, RWKV_WIDTH), lambda b, j: (b, j, 0))
    st = pl.BlockSpec((nb, HEAD_PAIRS, PAIR_LANES, PAIR_LANES), lambda b, j: (b, 0, 0, 0))
    return pl.pallas_call(
        _wkv_kernel,
        grid=(bsz // nb, nc),
        in_specs=[tile] * 6 + [st],
        out_specs=[tile, st],
        out_shape=[
            jax.ShapeDtypeStruct((bsz, t_len, RWKV_WIDTH), F32),
            jax.ShapeDtypeStruct((bsz, HEAD_PAIRS, PAIR_LANES, PAIR_LANES), F32),
        ],
        scratch_shapes=[pltpu.VMEM((nb, HEAD_PAIRS, PAIR_LANES, PAIR_LANES), F32)],
        compiler_params=_params(("parallel", "arbitrary")),
        name="wkv",
    )(r, lw, k, v, kk, kka, s0_bd)


S5_LANE_BLOCKS = S5_WIDTH // 128
S5_BLOCK_GROUPS = 128 // S5_CH
S5_BLOCK_STATE = S5_BLOCK_GROUPS * S5_STATE
S5_ROWS_PER_STEP = 256


def _s5_kernel(u_ref, x0_ref, wst_ref, wy_ref, tmat_ref, dec_ref, y_ref, x1_ref, z_scr, x_scr, *, nc):
    r = pl.program_id(1)
    rows = u_ref.shape[0]
    half = S5_BLOCK_STATE
    u2 = jnp.concatenate([u_ref[:, s, :] for s in range(S5_CHUNK)], axis=1).astype(BF16)
    z_scr[...] = _dot(u2, wst_ref[...])
    dre = dec_ref[:, :half]
    dim = dec_ref[:, half:]
    row0 = r * rows

    def body(i, carry):
        xre, xim = carry
        row = row0 + i
        b = row // nc
        start = (row % nc) == 0
        x0 = x0_ref[pl.ds(b, 1), :]
        xre = jnp.where(start, x0[:, :half], xre)
        xim = jnp.where(start, x0[:, half:], xim)
        zrow = z_scr[pl.ds(i, 1), :]
        z_scr[pl.ds(i, 1), :] = jnp.concatenate([xre, xim], axis=1)
        nre = dre * xre - dim * xim + zrow[:, :half]
        nim = dre * xim + dim * xre + zrow[:, half:]
        x1_ref[pl.ds(b, 1), :] = jnp.concatenate([nre, nim], axis=1)
        return nre, nim

    fin = lax.fori_loop(0, rows, body, (x_scr[0:1, :], x_scr[1:2, :]))
    x_scr[0:1, :] = fin[0]
    x_scr[1:2, :] = fin[1]
    y = _dot(u2, tmat_ref[...]) + _dot(z_scr[...].astype(BF16), wy_ref[...])
    for t in range(S5_CHUNK):
        y_ref[:, t, :] = y[:, t * 128:(t + 1) * 128]


def _s5(u3, x0, mats, nc):
    rows_all = u3.shape[0]
    bsz = rows_all // nc
    rows = min(S5_ROWS_PER_STEP, rows_all)
    width = S5_CHUNK * 128
    state = 2 * S5_BLOCK_STATE
    const = lambda shape: pl.BlockSpec((None,) + shape, lambda j, r: (j, 0, 0))
    tile = pl.BlockSpec((rows, S5_CHUNK, 128), lambda j, r: (r, 0, j))
    return pl.pallas_call(
        functools.partial(_s5_kernel, nc=nc),
        grid=(S5_LANE_BLOCKS, rows_all // rows),
        in_specs=[tile, const((bsz, state)), const((width, state)), const((state, width)),
                  const((width, width)), const((1, state))],
        out_specs=[tile, const((bsz, state))],
        out_shape=[jax.ShapeDtypeStruct(u3.shape, F32),
                   jax.ShapeDtypeStruct((S5_LANE_BLOCKS, bsz, state), F32)],
        scratch_shapes=[pltpu.VMEM((rows, state), F32), pltpu.VMEM((8, S5_BLOCK_STATE), F32)],
        compiler_params=_params(("parallel", "arbitrary")),
        name="s5",
    )(u3, x0, mats["wst"], mats["wy"], mats["tmat"], mats["dec"])


def _s5_matrices(a_re, a_im, log_dt, b_re, b_im, c_re, c_im, d):
    L = S5_CHUNK
    dt = jnp.exp(log_dt)[:, None]
    n = jnp.arange(L + 1, dtype=F32)[:, None, None]
    mag = jnp.exp(a_re * dt * n)
    ang = a_im * dt * n
    pw_re, pw_im = mag * jnp.cos(ang), mag * jnp.sin(ang)
    num_re, num_im = pw_re[1] - 1.0, pw_im[1]
    den = a_re * a_re + a_im * a_im
    f_re = (num_re * a_re + num_im * a_im) / den
    f_im = (num_im * a_re - num_re * a_im) / den
    bb_re = f_re[..., None] * b_re - f_im[..., None] * b_im
    bb_im = f_re[..., None] * b_im + f_im[..., None] * b_re
    pb_re = pw_re[..., None] * bb_re - pw_im[..., None] * bb_im
    pb_im = pw_re[..., None] * bb_im + pw_im[..., None] * bb_re
    kern = (jnp.einsum('gcp,ngpd->ngcd', c_re, pb_re[:L]) - jnp.einsum('gcp,ngpd->ngcd', c_im, pb_im[:L]))
    s_idx = jnp.arange(L)[:, None]
    t_idx = jnp.arange(L)[None, :]
    tk = kern[jnp.clip(t_idx - s_idx, 0, L - 1)]
    tk = jnp.where((t_idx >= s_idx)[:, :, None, None, None], tk, 0.0)
    skip = (jnp.eye(L, dtype=F32)[:, :, None, None, None] * d.reshape(1, 1, S5_GROUPS, S5_CH, 1)
            * jnp.eye(S5_CH, dtype=F32)[None, None, None])
    tk = tk + skip
    nb, bg = S5_LANE_BLOCKS, S5_BLOCK_GROUPS

    def block_diag(compact, row_inner, outer, inner):
        src = jnp.arange(outer * inner)
        dst = jnp.arange(outer * bg * inner)
        pick = ((src[:, None] // inner == dst[None, :] // (bg * inner))
                & (src[:, None] % inner == dst[None, :] % inner)).astype(BF16)
        wide = jnp.einsum('jrk,kn->jrn', compact.astype(BF16), pick, preferred_element_type=F32)
        row_group = (jnp.arange(compact.shape[1]) // row_inner) % bg
        keep = row_group[:, None] == (dst[None, :] // inner) % bg
        return jnp.where(keep, wide, 0.0).astype(BF16)

    tk = tk.reshape(L, L, nb, bg, S5_CH, S5_CH).transpose(2, 0, 3, 5, 1, 4)
    tmat = block_diag(tk.reshape(nb, L * 128, L * S5_CH), S5_CH, L, S5_CH)
    inj = jnp.stack([pb_re[:L][::-1], pb_im[:L][::-1]], axis=0)
    inj = inj.reshape(2, L, nb, bg, S5_STATE, S5_CH).transpose(2, 1, 3, 5, 0, 4)
    wst = block_diag(inj.reshape(nb, L * 128, 2 * S5_STATE), S5_CH, 2, S5_STATE)
    cp_re = c_re[None] * pw_re[1:, :, None, :] - c_im[None] * pw_im[1:, :, None, :]
    cp_im = c_re[None] * pw_im[1:, :, None, :] + c_im[None] * pw_re[1:, :, None, :]
    rd = jnp.stack([cp_re, -cp_im], axis=0).reshape(2, L, nb, bg, S5_CH, S5_STATE)
    rd = rd.transpose(2, 0, 3, 5, 1, 4)
    wy = block_diag(rd.reshape(nb, 2 * bg * S5_STATE, L * S5_CH), S5_STATE, L, S5_CH)
    dec = jnp.concatenate([pw_re[L].reshape(nb, 1, bg * S5_STATE), pw_im[L].reshape(nb, 1, bg * S5_STATE)],
                          axis=2)
    return {"wst": wst.astype(BF16), "wy": wy.astype(BF16), "tmat": tmat.astype(BF16), "dec": dec}


def _mix_out_kernel(o_ref, bonus_ref, g_ref, y_ref, x_ref, gnw_ref, gnb_ref, wglu_ref, bglu_ref,
                    wout_ref, ones_ref, x1_ref):
    ones_bd = ones_ref[...]
    o = o_ref[...]
    mu = _head_sum(o, ones_bd) * (1.0 / HEAD_DIM)
    d = o - mu
    var = _head_sum(d * d, ones_bd) * (1.0 / HEAD_DIM)
    o = d * lax.rsqrt(var + GN_EPS) * gnw_ref[...] + gnb_ref[...]
    o = (o + bonus_ref[...]) * g_ref[...]
    y = _gelu(y_ref[...])
    y = y * _sigmoid(_dot(y.astype(BF16), wglu_ref[...]) + bglu_ref[...])
    mixed = jnp.concatenate([o, y], axis=1).astype(BF16)
    x1_ref[...] = x_ref[...] + _dot(mixed, wout_ref[...])


def _mix_out(o, bonus, g, y, x2d, prm, tm):
    n = x2d.shape[0]
    tile = pl.BlockSpec((tm, RWKV_WIDTH), lambda i: (i, 0))
    row = pl.BlockSpec((1, RWKV_WIDTH), lambda i: (0, 0))
    sq = pl.BlockSpec((RWKV_WIDTH, RWKV_WIDTH), lambda i: (0, 0))
    return pl.pallas_call(
        _mix_out_kernel,
        grid=(n // tm,),
        in_specs=[tile, tile, tile, tile, pl.BlockSpec((tm, D_MODEL), lambda i: (i, 0)),
                  row, row, sq, row, pl.BlockSpec((D_MODEL, D_MODEL), lambda i: (0, 0)), sq],
        out_specs=pl.BlockSpec((tm, D_MODEL), lambda i: (i, 0)),
        out_shape=jax.ShapeDtypeStruct((n, D_MODEL), F32),
        compiler_params=_params(("parallel",)),
        name="mix_out",
    )(o, bonus, g, y, x2d, prm["gn_w"], prm["gn_b"], prm["w_glu"], prm["b_glu"], prm["w_out"],
      prm["ones_bd"])


def _sort16_pairs():
    def merge(lo, hi, r):
        step = 2 * r
        if step < hi - lo:
            yield from merge(lo, hi, step)
            yield from merge(lo + r, hi, step)
            yield from ((i, i + r) for i in range(lo + r, hi - r, step))
        else:
            yield (lo, lo + r)

    def sort(lo, hi):
        if hi > lo:
            mid = lo + (hi - lo) // 2
            yield from sort(lo, mid)
            yield from sort(mid + 1, hi)
            yield from merge(lo, hi, 1)

    return list(sort(0, PEER_TOPK - 1))


def _top16(x):
    sub = x.shape[0] // PEER_TOPK
    v = [x[k * sub:(k + 1) * sub, :] for k in range(PEER_TOPK)]
    for i, j in _sort16_pairs():
        v[i], v[j] = jnp.maximum(v[i], v[j]), jnp.minimum(v[i], v[j])
    vals = []
    for q in range(PEER_TOPK):
        m = jnp.max(v[0], axis=0, keepdims=True)
        vals.append(m)
        if q + 1 < PEER_TOPK:
            hit = v[0] == m
            for k in range(PEER_TOPK - 1 - q):
                v[k] = jnp.where(hit, v[k + 1], v[k])
    return vals


def _peer_route_kernel(x_ref, g_ref, wqt_ref, k1_ref, k2_ref,
                       h2t_ref, e1_ref, cnt_ref, rank2_ref, e2_ref):
    h2 = _rmsnorm(x_ref[...], g_ref[...])
    h2_b = h2.astype(BF16)
    h2t_ref[...] = jnp.transpose(h2).astype(BF16)
    qt = _dot_nt(wqt_ref[...], h2_b)
    for h in range(PEER_HEADS):
        q1 = qt[h * PEER_QDIM:h * PEER_QDIM + PEER_HALF].astype(BF16)
        q2 = qt[h * PEER_QDIM + PEER_HALF:(h + 1) * PEER_QDIM].astype(BF16)
        s1 = _dot(k1_ref[h], q1)
        s2 = _dot(k2_ref[h], q2)
        a1 = _top16(s1)
        a2 = _top16(s2)
        a1_mat = jnp.concatenate(a1, axis=0)
        a2_lo = jnp.concatenate(a2[:8], axis=0)
        cand = [a1[0] + a2_lo, a1[0] + jnp.concatenate(a2[8:], axis=0)]
        cand += [a1[i] + a2_lo for i in range(1, 8)]
        cand.append(a1_mat[8:] + a2[0])
        cand = jnp.concatenate(cand, axis=0)
        top = a1[0] + a2[0]
        zsum = jnp.zeros_like(top)
        thr = top
        for _ in range(PEER_TOPK):
            thr = jnp.max(cand, axis=0, keepdims=True)
            zsum = zsum + jnp.exp(thr - top)
            cand = jnp.where(cand == thr, -jnp.inf, cand)
        cnt = jnp.zeros_like(s1)
        for q in range(PEER_TOPK):
            reach = (a1_mat + a2[q]) >= thr
            b_q = jnp.min(jnp.where(reach, a1_mat, jnp.inf), axis=0, keepdims=True)
            cnt = jnp.where(s1 >= b_q, float(q + 1), cnt)
        rank2 = jnp.full(s2.shape, float(PEER_TOPK), F32)
        for q in reversed(range(PEER_TOPK)):
            rank2 = jnp.where(s2 >= a2[q], float(q), rank2)
        e1_ref[h] = jnp.exp(s1 - a1[0])
        cnt_ref[h] = cnt
        rank2_ref[h] = rank2.astype(BF16)
        e2_ref[h] = (jnp.exp(s2 - a2[0]) / zsum).astype(BF16)


def _peer_route(x1, ln2_g, wqt_bf, k1_bf, k2_bf, tt):
    n = x1.shape[0]
    keys = pl.BlockSpec((PEER_HEADS, N_KEYS, PEER_HALF), lambda i: (0, 0, 0))
    tok = pl.BlockSpec((PEER_HEADS, N_KEYS, tt), lambda i: (0, 0, i))
    per_key = lambda dt: jax.ShapeDtypeStruct((PEER_HEADS, N_KEYS, n), dt)
    return pl.pallas_call(
        _peer_route_kernel,
        grid=(n // tt,),
        in_specs=[
            pl.BlockSpec((tt, D_MODEL), lambda i: (i, 0)),
            pl.BlockSpec((1, D_MODEL), lambda i: (0, 0)),
            pl.BlockSpec((PEER_HEADS * PEER_QDIM, D_MODEL), lambda i: (0, 0)),
            keys, keys,
        ],
        out_specs=[pl.BlockSpec((D_MODEL, tt), lambda i: (0, i)), tok, tok, tok, tok],
        out_shape=[jax.ShapeDtypeStruct((D_MODEL, n), BF16), per_key(F32), per_key(F32), per_key(BF16),
                   per_key(BF16)],
        compiler_params=_params(("parallel",)),
        name="peer_route",
    )(x1, ln2_g, wqt_bf, k1_bf, k2_bf)


PEER_CHUNK = 1024
PEER_CHUNK_KEYS = PEER_CHUNK // N_KEYS
PEER_PIECE_ROWS = 16
PEER_PIECE_LANES = 128


def _peer_dense_kernel(h2t_ref, u_ref, vt_ref, e1_ref, cnt_ref, rank2_in_ref, e2_in_ref, x1_ref, g_ref,
                       y_ref, acc_scr, a0_scr, a1_scr, w_scr, tab_scr):
    c = pl.program_id(1)
    tt = h2t_ref.shape[1]

    @pl.when(c == 0)
    def _():
        acc_scr[...] = jnp.zeros_like(acc_scr)
        for sub in range(N_KEYS // PEER_PIECE_ROWS):
            second = slice(sub * PEER_PIECE_ROWS, (sub + 1) * PEER_PIECE_ROWS)
            for blk in range(tt // PEER_PIECE_LANES):
                lanes = slice(blk * PEER_PIECE_LANES, (blk + 1) * PEER_PIECE_LANES)
                for h in range(PEER_HEADS):
                    tab_scr[sub, blk, h, 0] = rank2_in_ref[h, second, lanes]
                    tab_scr[sub, blk, h, 1] = e2_in_ref[h, second, lanes]

    def activations(key, a_scr):
        a_scr[...] = _dot(u_ref[key], h2t_ref[...])

    def build(key, a_scr):
        base = pl.multiple_of(key * N_KEYS, N_KEYS)
        cnt_rows = [cnt_ref[h, pl.ds(key, 1), :] for h in range(PEER_HEADS)]
        e1_rows = [e1_ref[h, pl.ds(key, 1), :] for h in range(PEER_HEADS)]
        for blk in range(tt // PEER_PIECE_LANES):
            lanes = slice(blk * PEER_PIECE_LANES, (blk + 1) * PEER_PIECE_LANES)
            tile = (PEER_PIECE_ROWS, PEER_PIECE_LANES)
            cnt_b = [jnp.broadcast_to(cnt_rows[h][:, lanes], tile).astype(BF16) for h in range(PEER_HEADS)]
            e1_b = [jnp.broadcast_to(e1_rows[h][:, lanes], tile).astype(BF16) for h in range(PEER_HEADS)]
            for sub in range(N_KEYS // PEER_PIECE_ROWS):
                gate = None
                for h in range(PEER_HEADS):
                    sel = jnp.where(tab_scr[sub, blk, h, 0] < cnt_b[h], tab_scr[sub, blk, h, 1],
                                    jnp.zeros(tile, BF16))
                    term = sel * e1_b[h]
                    gate = term if gate is None else gate + term
                rows = slice(sub * PEER_PIECE_ROWS, (sub + 1) * PEER_PIECE_ROWS)
                out_rows = pl.ds(base + sub * PEER_PIECE_ROWS, PEER_PIECE_ROWS)
                w_scr[out_rows, lanes] = _gelu(a_scr[rows, lanes]).astype(BF16) * gate

    activations(0, a0_scr)

    def body(it, carry):
        even = 2 * it
        odd = even + 1
        activations(odd, a1_scr)
        build(even, a0_scr)
        activations(jnp.minimum(odd + 1, PEER_CHUNK_KEYS - 1), a0_scr)
        build(odd, a1_scr)
        return carry

    lax.fori_loop(0, PEER_CHUNK_KEYS // 2, body, 0)
    acc_scr[...] += _dot(vt_ref[...], w_scr[...])

    @pl.when(c == pl.num_programs(1) - 1)
    def _():
        x2 = x1_ref[...] + jnp.transpose(acc_scr[...])
        y_ref[...] = _rmsnorm(x2, g_ref[...])


def _peer_dense(h2t, u_bf, vt_bf, e1, cnt, rank2, e2, x1, lnf_g, tt):
    n = x1.shape[0]
    nchunks = N_EXPERTS // PEER_CHUNK
    by_chunk = lambda z: z.reshape(PEER_HEADS, nchunks, PEER_CHUNK_KEYS, n)
    tok_all = pl.BlockSpec((PEER_HEADS, N_KEYS, tt), lambda i, c: (0, 0, i))
    tok_chunk = pl.BlockSpec((PEER_HEADS, None, PEER_CHUNK_KEYS, tt), lambda i, c: (0, c, 0, i))
    return pl.pallas_call(
        _peer_dense_kernel,
        grid=(n // tt, nchunks),
        in_specs=[
            pl.BlockSpec((D_MODEL, tt), lambda i, c: (0, i)),
            pl.BlockSpec((PEER_CHUNK_KEYS, N_KEYS, D_MODEL), lambda i, c: (c, 0, 0)),
            pl.BlockSpec((None, D_MODEL, PEER_CHUNK), lambda i, c: (c, 0, 0)),
            tok_chunk, tok_chunk, tok_all, tok_all,
            pl.BlockSpec((tt, D_MODEL), lambda i, c: (i, 0)),
            pl.BlockSpec((1, D_MODEL), lambda i, c: (0, 0)),
        ],
        out_specs=pl.BlockSpec((tt, D_MODEL), lambda i, c: (i, 0)),
        out_shape=jax.ShapeDtypeStruct((n, D_MODEL), F32),
        scratch_shapes=[pltpu.VMEM((D_MODEL, tt), F32),
                        pltpu.VMEM((N_KEYS, tt), F32), pltpu.VMEM((N_KEYS, tt), F32),
                        pltpu.VMEM((PEER_CHUNK, tt), BF16),
                        pltpu.VMEM((N_KEYS // PEER_PIECE_ROWS, tt // PEER_PIECE_LANES, PEER_HEADS, 2,
                                    PEER_PIECE_ROWS, PEER_PIECE_LANES), BF16)],
        compiler_params=_params(("parallel", "arbitrary")),
        name="peer_dense",
    )(h2t, u_bf, vt_bf, by_chunk(e1), by_chunk(cnt), rank2, e2, x1, lnf_g)


def _pick_tile(n, target):
    t = min(n, target)
    while n % t:
        t //= 2
    return t


def _layer(x, wkv0, shift0, s5re0, s5im0, prm, s5m, lnf_g):
    bsz, t_len, _ = x.shape
    n = bsz * t_len
    x2d = x.reshape(n, D_MODEL)
    tm = _pick_tile(t_len, 512)
    prw, u = _in_proj(x2d, prm["ln1_g"], prm["w_in"], tm)

    tiles_per_seq = t_len // tm
    prw3 = prw.reshape(bsz, t_len, RWKV_PROJ)
    last_rows = prw3[:, tm - 1::tm, :]
    bound = jnp.concatenate([shift0, last_rows[:, :tiles_per_seq - 1, :]], axis=1)
    bound = bound.reshape(bsz * tiles_per_seq, 1, RWKV_PROJ)
    r, lw, k, v, kk, kka, g, bonus = _rwkv_prep(prw, bound, prm, tm)

    t_pad = -(-t_len // WKV_CHUNK) * WKV_CHUNK
    seq3 = lambda z: jnp.pad(z.reshape(bsz, t_len, RWKV_WIDTH), ((0, 0), (0, t_pad - t_len), (0, 0)))
    st = wkv0.reshape(bsz, HEAD_PAIRS, 2, HEAD_DIM, HEAD_DIM).transpose(0, 1, 2, 4, 3)
    eye2 = jnp.eye(2, dtype=F32)
    s0_bd = jnp.einsum('bphkv,hg->bphkgv', st, eye2).reshape(bsz, HEAD_PAIRS, PAIR_LANES, PAIR_LANES)
    o, s1_bd = _wkv(seq3(r), seq3(lw), seq3(k), seq3(v), seq3(kk), seq3(kka), s0_bd)
    o = o[:, :t_len].reshape(n, RWKV_WIDTH)
    s1 = s1_bd.reshape(bsz, HEAD_PAIRS, 2, HEAD_DIM, 2, HEAD_DIM)
    s1 = jnp.stack([s1[:, :, 0, :, 0, :], s1[:, :, 1, :, 1, :]], axis=2)
    wkv1 = s1.transpose(0, 1, 2, 4, 3).reshape(bsz, RWKV_HEADS, HEAD_DIM, HEAD_DIM)

    nc = t_len // S5_CHUNK
    blk = lambda z: z.reshape(bsz, S5_LANE_BLOCKS, S5_BLOCK_STATE).transpose(1, 0, 2)
    x0 = jnp.concatenate([blk(s5re0), blk(s5im0)], axis=2)
    y3, x1s = _s5(u.reshape(bsz * nc, S5_CHUNK, S5_WIDTH), x0, s5m, nc)
    y = y3.reshape(n, S5_WIDTH)
    unblk = lambda z: z.transpose(1, 0, 2).reshape(bsz, S5_GROUPS, S5_STATE)
    s5re1, s5im1 = unblk(x1s[:, :, :S5_BLOCK_STATE]), unblk(x1s[:, :, S5_BLOCK_STATE:])

    x1 = _mix_out(o, bonus, g, y, x2d, prm, tm)

    tt = _pick_tile(n, 512)
    h2t, e1, cnt, rank2, e2 = _peer_route(x1, prm["ln2_g"], prm["w_qt"], prm["keys1"], prm["keys2"], tt)
    yout = _peer_dense(h2t, prm["peer_u"], prm["peer_vt"], e1, cnt, rank2, e2, x1, lnf_g, tt)
    shift1 = prw3[:, -1:, :]
    return yout.reshape(bsz, t_len, D_MODEL), wkv1, shift1, s5re1, s5im1


def kernel(x_prompt, x_sample, state_wkv, state_shift, state_s5_re, state_s5_im, ln1_g, w_in, rwkv_mu, rwkv_w0, rwkv_w2, rwkv_a0, rwkv_a2, rwkv_g2, rwkv_k_k, rwkv_k_a, rwkv_r_k, rwkv_gn_w, rwkv_gn_b, s5_a_re, s5_a_im, s5_log_dt, s5_b_re, s5_b_im, s5_c_re, s5_c_im, s5_d, s5_w_glu, s5_b_glu, w_out, ln2_g, peer_w_q, peer_keys1, peer_keys2, peer_u, peer_v, lnf_g):
    depth = w_in.shape[0]
    assert depth == 1
    l = 0
    row = lambda z: z.reshape(1, -1).astype(F32)
    lora = jnp.zeros((LORA_WIDTH, 3 * RWKV_WIDTH), F32)
    lora = lora.at[:64, :RWKV_WIDTH].set(rwkv_w2[l])
    lora = lora.at[64:128, RWKV_WIDTH:2 * RWKV_WIDTH].set(rwkv_a2[l])
    lora = lora.at[128:, 2 * RWKV_WIDTH:].set(rwkv_g2[l])
    lane = jnp.arange(RWKV_WIDTH)
    ones_bd = (lane[:, None] // HEAD_DIM == lane[None, :] // HEAD_DIM).astype(BF16)
    prm = {
        "ln1_g": row(ln1_g[l]), "w_in": w_in[l].astype(BF16), "mu": row(rwkv_mu[l]),
        "w0": row(rwkv_w0[l]), "a0": row(rwkv_a0[l]), "lora": lora.astype(BF16),
        "k_k": row(rwkv_k_k[l]), "k_a": row(rwkv_k_a[l]), "r_k": row(rwkv_r_k[l]),
        "gn_w": row(rwkv_gn_w[l]), "gn_b": row(rwkv_gn_b[l]), "ones_bd": ones_bd,
        "w_glu": s5_w_glu[l].astype(BF16), "b_glu": row(s5_b_glu[l]), "w_out": w_out[l].astype(BF16),
        "ln2_g": row(ln2_g[l]), "w_qt": peer_w_q[l].T.astype(BF16),
        "keys1": peer_keys1[l].astype(BF16), "keys2": peer_keys2[l].astype(BF16),
        "peer_u": peer_u[l].astype(BF16).reshape(N_KEYS, N_KEYS, D_MODEL),
        "peer_vt": peer_v[l].astype(BF16).reshape(N_EXPERTS // PEER_CHUNK, PEER_CHUNK, D_MODEL)
        .transpose(0, 2, 1),
    }
    s5m = _s5_matrices(s5_a_re[l], s5_a_im[l], s5_log_dt[l], s5_b_re[l], s5_b_im[l], s5_c_re[l],
                       s5_c_im[l], s5_d[l])
    lnf = row(lnf_g)
    bp = x_prompt.shape[0]
    zeros = lambda *s: jnp.zeros(s, F32)
    yp, a1, a2, a3, a4 = _layer(
        x_prompt, zeros(bp, RWKV_HEADS, HEAD_DIM, HEAD_DIM), zeros(bp, 1, RWKV_PROJ),
        zeros(bp, S5_GROUPS, S5_STATE), zeros(bp, S5_GROUPS, S5_STATE), prm, s5m, lnf)
    ys, b1, b2, b3, b4 = _layer(x_sample, state_wkv[l], state_shift[l], state_s5_re[l], state_s5_im[l],
                                prm, s5m, lnf)
    st = lambda z: z[None]
    return (yp, ys, st(a1), st(a2), st(a3), st(a4), st(b1), st(b2), st(b3), st(b4))
```

```python
import functools
import math

import jax
import jax.numpy as jnp
from jax import lax
from jax.experimental import pallas as pl
from jax.experimental.pallas import tpu as pltpu

F32 = jnp.float32
BF16 = jnp.bfloat16

D_MODEL = 1024
RWKV_WIDTH = 512
HEAD_DIM = 64
RWKV_HEADS = 8
HEAD_PAIRS = RWKV_HEADS // 2
PAIR_LANES = 2 * HEAD_DIM
LORA_WIDTH = 256
RWKV_PROJ = 3 * RWKV_WIDTH + LORA_WIDTH
S5_WIDTH = 512
S5_CH = 16
S5_GROUPS = 32
S5_STATE = 64
S5_CHUNK = 16
IN_PROJ = RWKV_PROJ + S5_WIDTH
PEER_HEADS = 8
N_KEYS = 128
N_EXPERTS = N_KEYS * N_KEYS
PEER_TOPK = 16
PEER_QDIM = 256
PEER_HALF = 128
NORM_EPS = 1e-6
GN_EPS = HEAD_DIM * 1e-5
WKV_CHUNK = 64

VMEM_LIMIT_BYTES = 48 * 1024 * 1024


def _params(semantics, flags=None):
    return pltpu.CompilerParams(dimension_semantics=semantics, vmem_limit_bytes=VMEM_LIMIT_BYTES,
                                flags=flags)


def _dot(a, b):
    return jnp.dot(a, b, preferred_element_type=F32)


def _dot_nt(a, b):
    return lax.dot_general(a, b, (((1,), (1,)), ((), ())), preferred_element_type=F32)


def _dot_tn(a, b):
    return lax.dot_general(a, b, (((0,), (0,)), ((), ())), preferred_element_type=F32)


def _split2(x):
    hi = x.astype(BF16)
    lo = (x - hi.astype(F32)).astype(BF16)
    return hi, lo


def _split3(x):
    hi = x.astype(BF16)
    r1 = x - hi.astype(F32)
    mid = r1.astype(BF16)
    lo = (r1 - mid.astype(F32)).astype(BF16)
    return hi, mid, lo


def _head_sum(x, ones_bd):
    hi, lo = _split2(x)
    return _dot(hi, ones_bd) + _dot(lo, ones_bd)


def _rmsnorm(x, g):
    return x * lax.rsqrt(jnp.mean(x * x, axis=-1, keepdims=True) + NORM_EPS) * g


def _gelu(x):
    c = math.sqrt(2.0 / math.pi)
    half = 0.5 * x
    return half * jnp.tanh(x * (c + (c * 0.044715) * (x * x))) + half


def _sigmoid(x):
    return 1.0 / (1.0 + jnp.exp(-x))


def _in_proj_kernel(x_ref, g_ref, w_ref, prw_ref, u_ref):
    h = _rmsnorm(x_ref[...], g_ref[...]).astype(BF16)
    p = _dot(h, w_ref[...])
    prw_ref[...] = p[:, :RWKV_PROJ]
    u_ref[...] = p[:, RWKV_PROJ:]


def _in_proj(x2d, ln1_g, w_in_bf, tm):
    n = x2d.shape[0]
    return pl.pallas_call(
        _in_proj_kernel,
        grid=(n // tm,),
        in_specs=[
            pl.BlockSpec((tm, D_MODEL), lambda i: (i, 0)),
            pl.BlockSpec((1, D_MODEL), lambda i: (0, 0)),
            pl.BlockSpec((D_MODEL, IN_PROJ), lambda i: (0, 0)),
        ],
        out_specs=[
            pl.BlockSpec((tm, RWKV_PROJ), lambda i: (i, 0)),
            pl.BlockSpec((tm, S5_WIDTH), lambda i: (i, 0)),
        ],
        out_shape=[
            jax.ShapeDtypeStruct((n, RWKV_PROJ), F32),
            jax.ShapeDtypeStruct((n, S5_WIDTH), F32),
        ],
        compiler_params=_params(("parallel",)),
        name="in_proj",
    )(x2d, ln1_g, w_in_bf)


def _rwkv_prep_kernel(p_ref, bound_ref, mu_ref, w0_ref, a0_ref, lora_ref, kk_ref_, ka_ref, rk_ref,
                      ones_ref, r_out, lw_out, k_out, v_out, kk_out, kka_out, g_out, bonus_out):
    p = p_ref[...]
    tm = p.shape[0]
    row = lax.broadcasted_iota(jnp.int32, p.shape, 0)
    p_prev = jnp.where(row == 0, bound_ref[0], pltpu.roll(p, 1, axis=0))
    m = p + (p_prev - p) * mu_ref[...]
    r = m[:, :RWKV_WIDTH]
    k = m[:, RWKV_WIDTH:2 * RWKV_WIDTH]
    v = m[:, 2 * RWKV_WIDTH:3 * RWKV_WIDTH]
    z = m[:, 3 * RWKV_WIDTH:]
    lane = lax.broadcasted_iota(jnp.int32, (tm, LORA_WIDTH), 1)
    feat = jnp.where(lane < 64, jnp.tanh(z), jnp.where(lane < 128, z, _sigmoid(z)))
    lo = _dot(feat.astype(BF16), lora_ref[...])
    wl = -(w0_ref[...] + lo[:, :RWKV_WIDTH])
    softplus = jnp.maximum(wl, 0.0) + jnp.log(1.0 + jnp.exp(-jnp.abs(wl)))
    lw_out[...] = -jnp.exp(-softplus - 0.5)
    a = _sigmoid(a0_ref[...] + lo[:, RWKV_WIDTH:2 * RWKV_WIDTH])
    g_out[...] = lo[:, 2 * RWKV_WIDTH:]
    ones_bd = ones_ref[...]
    kk = k * kk_ref_[...]
    nrm = jnp.sqrt(_head_sum(kk * kk, ones_bd))
    kk = kk / jnp.maximum(nrm, 1e-12)
    k2 = k * (1.0 + (a - 1.0) * ka_ref[...])
    r_out[...] = r
    k_out[...] = k2
    v_out[...] = v
    kk_out[...] = kk
    kka_out[...] = kk * a
    bonus_out[...] = _head_sum(r * k2 * rk_ref[...], ones_bd) * v


def _rwkv_prep(prw, bound, prm, tm):
    n = prw.shape[0]
    row = lambda width: pl.BlockSpec((1, width), lambda i: (0, 0))
    tile = pl.BlockSpec((tm, RWKV_WIDTH), lambda i: (i, 0))
    return pl.pallas_call(
        _rwkv_prep_kernel,
        grid=(n // tm,),
        in_specs=[
            pl.BlockSpec((tm, RWKV_PROJ), lambda i: (i, 0)),
            pl.BlockSpec((1, 1, RWKV_PROJ), lambda i: (i, 0, 0)),
            row(RWKV_PROJ), row(RWKV_WIDTH), row(RWKV_WIDTH),
            pl.BlockSpec((LORA_WIDTH, 3 * RWKV_WIDTH), lambda i: (0, 0)),
            row(RWKV_WIDTH), row(RWKV_WIDTH), row(RWKV_WIDTH),
            pl.BlockSpec((RWKV_WIDTH, RWKV_WIDTH), lambda i: (0, 0)),
        ],
        out_specs=[tile] * 8,
        out_shape=[jax.ShapeDtypeStruct((n, RWKV_WIDTH), F32)] * 8,
        compiler_params=_params(("parallel",)),
        name="rwkv_prep",
    )(prw, bound, prm["mu"], prm["w0"], prm["a0"], prm["lora"], prm["k_k"], prm["k_a"], prm["r_k"],
      prm["ones_bd"])


def _wkv_units(units, s_list, tri, bd_strict, bd_incl, eye2, eye_s, lane_lo):
    seq = units[0][0].shape[0]
    two = 2 * seq
    n = len(units)
    rng = range(n)

    def stack(z):
        return jnp.concatenate([jnp.where(lane_lo, z, 0.0), jnp.where(lane_lo, 0.0, z)], axis=0)

    cl = []
    for (r, lw, k, v, kk, kka) in units:
        hi, mid, lo = _split3(lw)
        cl.append(_dot(tri, hi) + _dot(tri, mid) + _dot(tri, lo))
    gam = [jnp.exp(c) for c in cl]
    gam_inv = [jnp.exp(-c) for c in cl]
    gam_prev = [jnp.exp(cl[i] - units[i][1]) for i in rng]
    gam_last = [g[seq - 1:seq, :] for g in gam]
    a_t = [-units[i][4] * gam_prev[i] for i in rng]
    r_t = [units[i][0] * gam[i] for i in rng]
    b_t = [units[i][5] * gam_inv[i] for i in rng]
    k_t = [units[i][2] * gam_inv[i] for i in rng]
    a_st = [stack(z) for z in a_t]
    r_st = [stack(z) for z in r_t]
    v_st_b = [stack(u[3]).astype(BF16) for u in units]
    bh_st = [stack(b_t[i] * gam_last[i]).astype(BF16) for i in rng]
    kh_st = [stack(k_t[i] * gam_last[i]).astype(BF16) for i in rng]
    x = [_dot_nt(jnp.concatenate([a_st[i], r_st[i]], axis=0).astype(BF16),
                 jnp.concatenate([b_t[i], b_t[i], k_t[i], k_t[i]], axis=0).astype(BF16)) for i in rng]
    m_ab = [jnp.where(bd_strict, z[:two, :two], 0.0) for z in x]
    m_akv = [_dot(jnp.where(bd_strict, x[i][:two, two:], 0.0).astype(BF16), v_st_b[i]) for i in rng]
    n_cat = [jnp.concatenate([jnp.where(bd_incl, z[two:, :two], 0.0),
                              jnp.where(bd_incl, z[two:, two:], 0.0)], axis=1).astype(BF16) for z in x]
    t_inv = [eye2 + m for m in m_ab]
    pw = m_ab
    steps = 1
    while steps * 2 < seq:
        pw_b = [p.astype(BF16) for p in pw]
        pw = [_dot(p, p) for p in pw_b]
        t_inv = [_dot(t_inv[i].astype(BF16), (eye2 + pw[i]).astype(BF16)) for i in rng]
        steps *= 2
    ty = [_dot(t_inv[i].astype(BF16), jnp.concatenate([a_st[i], m_akv[i]], axis=1).astype(BF16))
          for i in rng]
    ah_b = [z[:, :PAIR_LANES].astype(BF16) for z in ty]
    uh_b = [z[:, PAIR_LANES:].astype(BF16) for z in ty]
    p_t = [_dot_tn(bh_st[i], ah_b[i]) + jnp.where(eye_s, gam_last[i], 0.0) for i in rng]
    q_t = [_dot_tn(jnp.concatenate([bh_st[i], kh_st[i]], axis=0),
                   jnp.concatenate([uh_b[i], v_st_b[i]], axis=0)) for i in rng]
    ro = [_dot(n_cat[i], jnp.concatenate(
        [jnp.concatenate([ah_b[i], uh_b[i]], axis=1),
         jnp.concatenate([jnp.zeros_like(v_st_b[i]), v_st_b[i]], axis=1)], axis=0)) for i in rng]
    outs, states = [], []
    for i in rng:
        s_hi, s_lo = _split2(s_list[i])
        rh_hi, rh_lo = _split2(r_st[i] + ro[i][:, :PAIR_LANES])
        o_st = _dot(rh_hi, s_hi) + _dot(rh_hi, s_lo) + _dot(rh_lo, s_hi) + ro[i][:, PAIR_LANES:]
        outs.append(o_st[:seq] + o_st[seq:])
        p_hi, p_lo = _split2(p_t[i])
        states.append(_dot(p_hi, s_hi) + _dot(p_hi, s_lo) + _dot(p_lo, s_hi) + q_t[i])
    return outs, states


def _wkv_kernel(r_ref, lw_ref, k_ref, v_ref, kk_ref, kka_ref, s0_ref, o_ref, s1_ref, s_scr):
    j = pl.program_id(1)
    nb, seq = r_ref.shape[0], r_ref.shape[1]
    two = 2 * seq

    @pl.when(j == 0)
    def _():
        s_scr[...] = s0_ref[...]

    ri = lax.broadcasted_iota(jnp.int32, (seq, seq), 0)
    ci = lax.broadcasted_iota(jnp.int32, (seq, seq), 1)
    tri = (ri >= ci).astype(BF16)
    r2 = lax.broadcasted_iota(jnp.int32, (two, two), 0)
    c2 = lax.broadcasted_iota(jnp.int32, (two, two), 1)
    same = (r2 // seq) == (c2 // seq)
    bd_strict = same & ((r2 % seq) > (c2 % seq))
    bd_incl = same & ((r2 % seq) >= (c2 % seq))
    eye2 = (r2 == c2).astype(F32)
    rs = lax.broadcasted_iota(jnp.int32, (PAIR_LANES, PAIR_LANES), 0)
    cs = lax.broadcasted_iota(jnp.int32, (PAIR_LANES, PAIR_LANES), 1)
    eye_s = rs == cs
    lane_lo = lax.broadcasted_iota(jnp.int32, (seq, PAIR_LANES), 1) < HEAD_DIM

    ids = [(b, pair) for b in range(nb) for pair in range(HEAD_PAIRS)]
    lanes = lambda pair: slice(pair * PAIR_LANES, (pair + 1) * PAIR_LANES)
    units = [tuple(ref[b, :, lanes(pair)] for ref in (r_ref, lw_ref, k_ref, v_ref, kk_ref, kka_ref))
             for (b, pair) in ids]
    s_list = [s_scr[b, pair] for (b, pair) in ids]
    outs, states = _wkv_units(units, s_list, tri, bd_strict, bd_incl, eye2, eye_s, lane_lo)
    for (b, pair), o, s_new in zip(ids, outs, states):
        o_ref[b, :, lanes(pair)] = o
        s_scr[b, pair] = s_new

    @pl.when(j == pl.num_programs(1) - 1)
    def _():
        s1_ref[...] = s_scr[...]


WKV_BATCH_PER_STEP = 2


def _wkv(r, lw, k, v, kk, kka, s0_bd):
    bsz, t_len, _ = r.shape
    nc = t_len // WKV_CHUNK
    nb = WKV_BATCH_PER_STEP
    tile = pl.BlockSpec((nb, WKV_CHUNK, RWKV_WIDTH), lambda b, j: (b, j, 0))
    st = pl.BlockSpec((nb, HEAD_PAIRS, PAIR_LANES, PAIR_LANES), lambda b, j: (b, 0, 0, 0))
    return pl.pallas_call(
        _wkv_kernel,
        grid=(bsz // nb, nc),
        in_specs=[tile] * 6 + [st],
        out_specs=[tile, st],
        out_shape=[
            jax.ShapeDtypeStruct((bsz, t_len, RWKV_WIDTH), F32),
            jax.ShapeDtypeStruct((bsz, HEAD_PAIRS, PAIR_LANES, PAIR_LANES), F32),
        ],
        scratch_shapes=[pltpu.VMEM((nb, HEAD_PAIRS, PAIR_LANES, PAIR_LANES), F32)],
        compiler_params=_params(("parallel", "arbitrary")),
        name="wkv",
    )(r, lw, k, v, kk, kka, s0_bd)


S5_LANE_BLOCKS = S5_WIDTH // 128
S5_BLOCK_GROUPS = 128 // S5_CH
S5_BLOCK_STATE = S5_BLOCK_GROUPS * S5_STATE
S5_ROWS_PER_STEP = 256


def _s5_kernel(u_ref, x0_ref, wst_ref, wy_ref, tmat_ref, dec_ref, y_ref, x1_ref, z_scr, x_scr, *, nc):
    r = pl.program_id(1)
    rows = u_ref.shape[0]
    half = S5_BLOCK_STATE
    u2 = jnp.concatenate([u_ref[:, s, :] for s in range(S5_CHUNK)], axis=1).astype(BF16)
    z_scr[...] = _dot(u2, wst_ref[...])
    dre = dec_ref[:, :half]
    dim = dec_ref[:, half:]
    row0 = r * rows

    def body(i, carry):
        xre, xim = carry
        row = row0 + i
        b = row // nc
        start = (row % nc) == 0
        x0 = x0_ref[pl.ds(b, 1), :]
        xre = jnp.where(start, x0[:, :half], xre)
        xim = jnp.where(start, x0[:, half:], xim)
        zrow = z_scr[pl.ds(i, 1), :]
        z_scr[pl.ds(i, 1), :] = jnp.concatenate([xre, xim], axis=1)
        nre = dre * xre - dim * xim + zrow[:, :half]
        nim = dre * xim + dim * xre + zrow[:, half:]
        x1_ref[pl.ds(b, 1), :] = jnp.concatenate([nre, nim], axis=1)
        return nre, nim

    fin = lax.fori_loop(0, rows, body, (x_scr[0:1, :], x_scr[1:2, :]))
    x_scr[0:1, :] = fin[0]
    x_scr[1:2, :] = fin[1]
    y = _dot(u2, tmat_ref[...]) + _dot(z_scr[...].astype(BF16), wy_ref[...])
    for t in range(S5_CHUNK):
        y_ref[:, t, :] = y[:, t * 128:(t + 1) * 128]


def _s5(u3, x0, mats, nc):
    rows_all = u3.shape[0]
    bsz = rows_all // nc
    rows = min(S5_ROWS_PER_STEP, rows_all)
    width = S5_CHUNK * 128
    state = 2 * S5_BLOCK_STATE
    const = lambda shape: pl.BlockSpec((None,) + shape, lambda j, r: (j, 0, 0))
    tile = pl.BlockSpec((rows, S5_CHUNK, 128), lambda j, r: (r, 0, j))
    return pl.pallas_call(
        functools.partial(_s5_kernel, nc=nc),
        grid=(S5_LANE_BLOCKS, rows_all // rows),
        in_specs=[tile, const((bsz, state)), const((width, state)), const((state, width)),
                  const((width, width)), const((1, state))],
        out_specs=[tile, const((bsz, state))],
        out_shape=[jax.ShapeDtypeStruct(u3.shape, F32),
                   jax.ShapeDtypeStruct((S5_LANE_BLOCKS, bsz, state), F32)],
        scratch_shapes=[pltpu.VMEM((rows, state), F32), pltpu.VMEM((8, S5_BLOCK_STATE), F32)],
        compiler_params=_params(("parallel", "arbitrary")),
        name="s5",
    )(u3, x0, mats["wst"], mats["wy"], mats["tmat"], mats["dec"])


def _s5_matrices(a_re, a_im, log_dt, b_re, b_im, c_re, c_im, d):
    L = S5_CHUNK
    dt = jnp.exp(log_dt)[:, None]
    n = jnp.arange(L + 1, dtype=F32)[:, None, None]
    mag = jnp.exp(a_re * dt * n)
    ang = a_im * dt * n
    pw_re, pw_im = mag * jnp.cos(ang), mag * jnp.sin(ang)
    num_re, num_im = pw_re[1] - 1.0, pw_im[1]
    den = a_re * a_re + a_im * a_im
    f_re = (num_re * a_re + num_im * a_im) / den
    f_im = (num_im * a_re - num_re * a_im) / den
    bb_re = f_re[..., None] * b_re - f_im[..., None] * b_im
    bb_im = f_re[..., None] * b_im + f_im[..., None] * b_re
    pb_re = pw_re[..., None] * bb_re - pw_im[..., None] * bb_im
    pb_im = pw_re[..., None] * bb_im + pw_im[..., None] * bb_re
    kern = (jnp.einsum('gcp,ngpd->ngcd', c_re, pb_re[:L]) - jnp.einsum('gcp,ngpd->ngcd', c_im, pb_im[:L]))
    s_idx = jnp.arange(L)[:, None]
    t_idx = jnp.arange(L)[None, :]
    tk = kern[jnp.clip(t_idx - s_idx, 0, L - 1)]
    tk = jnp.where((t_idx >= s_idx)[:, :, None, None, None], tk, 0.0)
    skip = (jnp.eye(L, dtype=F32)[:, :, None, None, None] * d.reshape(1, 1, S5_GROUPS, S5_CH, 1)
            * jnp.eye(S5_CH, dtype=F32)[None, None, None])
    tk = tk + skip
    nb, bg = S5_LANE_BLOCKS, S5_BLOCK_GROUPS

    def block_diag(compact, row_inner, outer, inner):
        src = jnp.arange(outer * inner)
        dst = jnp.arange(outer * bg * inner)
        pick = ((src[:, None] // inner == dst[None, :] // (bg * inner))
                & (src[:, None] % inner == dst[None, :] % inner)).astype(BF16)
        wide = jnp.einsum('jrk,kn->jrn', compact.astype(BF16), pick, preferred_element_type=F32)
        row_group = (jnp.arange(compact.shape[1]) // row_inner) % bg
        keep = row_group[:, None] == (dst[None, :] // inner) % bg
        return jnp.where(keep, wide, 0.0).astype(BF16)

    tk = tk.reshape(L, L, nb, bg, S5_CH, S5_CH).transpose(2, 0, 3, 5, 1, 4)
    tmat = block_diag(tk.reshape(nb, L * 128, L * S5_CH), S5_CH, L, S5_CH)
    inj = jnp.stack([pb_re[:L][::-1], pb_im[:L][::-1]], axis=0)
    inj = inj.reshape(2, L, nb, bg, S5_STATE, S5_CH).transpose(2, 1, 3, 5, 0, 4)
    wst = block_diag(inj.reshape(nb, L * 128, 2 * S5_STATE), S5_CH, 2, S5_STATE)
    cp_re = c_re[None] * pw_re[1:, :, None, :] - c_im[None] * pw_im[1:, :, None, :]
    cp_im = c_re[None] * pw_im[1:, :, None, :] + c_im[None] * pw_re[1:, :, None, :]
    rd = jnp.stack([cp_re, -cp_im], axis=0).reshape(2, L, nb, bg, S5_CH, S5_STATE)
    rd = rd.transpose(2, 0, 3, 5, 1, 4)
    wy = block_diag(rd.reshape(nb, 2 * bg * S5_STATE, L * S5_CH), S5_STATE, L, S5_CH)
    dec = jnp.concatenate([pw_re[L].reshape(nb, 1, bg * S5_STATE), pw_im[L].reshape(nb, 1, bg * S5_STATE)],
                          axis=2)
    return {"wst": wst.astype(BF16), "wy": wy.astype(BF16), "tmat": tmat.astype(BF16), "dec": dec}


def _mix_out_kernel(o_ref, bonus_ref, g_ref, y_ref, x_ref, gnw_ref, gnb_ref, wglu_ref, bglu_ref,
                    wout_ref, ones_ref, x1_ref):
    ones_bd = ones_ref[...]
    o = o_ref[...]
    mu = _head_sum(o, ones_bd) * (1.0 / HEAD_DIM)
    d = o - mu
    var = _head_sum(d * d, ones_bd) * (1.0 / HEAD_DIM)
    o = d * lax.rsqrt(var + GN_EPS) * gnw_ref[...] + gnb_ref[...]
    o = (o + bonus_ref[...]) * g_ref[...]
    y = _gelu(y_ref[...])
    y = y * _sigmoid(_dot(y.astype(BF16), wglu_ref[...]) + bglu_ref[...])
    mixed = jnp.concatenate([o, y], axis=1).astype(BF16)
    x1_ref[...] = x_ref[...] + _dot(mixed, wout_ref[...])


def _mix_out(o, bonus, g, y, x2d, prm, tm):
    n = x2d.shape[0]
    tile = pl.BlockSpec((tm, RWKV_WIDTH), lambda i: (i, 0))
    row = pl.BlockSpec((1, RWKV_WIDTH), lambda i: (0, 0))
    sq = pl.BlockSpec((RWKV_WIDTH, RWKV_WIDTH), lambda i: (0, 0))
    return pl.pallas_call(
        _mix_out_kernel,
        grid=(n // tm,),
        in_specs=[tile, tile, tile, tile, pl.BlockSpec((tm, D_MODEL), lambda i: (i, 0)),
                  row, row, sq, row, pl.BlockSpec((D_MODEL, D_MODEL), lambda i: (0, 0)), sq],
        out_specs=pl.BlockSpec((tm, D_MODEL), lambda i: (i, 0)),
        out_shape=jax.ShapeDtypeStruct((n, D_MODEL), F32),
        compiler_params=_params(("parallel",)),
        name="mix_out",
    )(o, bonus, g, y, x2d, prm["gn_w"], prm["gn_b"], prm["w_glu"], prm["b_glu"], prm["w_out"],
      prm["ones_bd"])


def _sort16_pairs():
    def merge(lo, hi, r):
        step = 2 * r
        if step < hi - lo:
            yield from merge(lo, hi, step)
            yield from merge(lo + r, hi, step)
            yield from ((i, i + r) for i in range(lo + r, hi - r, step))
        else:
            yield (lo, lo + r)

    def sort(lo, hi):
        if hi > lo:
            mid = lo + (hi - lo) // 2
            yield from sort(lo, mid)
            yield from sort(mid + 1, hi)
            yield from merge(lo, hi, 1)

    return list(sort(0, PEER_TOPK - 1))


def _top16(x):
    sub = x.shape[0] // PEER_TOPK
    v = [x[k * sub:(k + 1) * sub, :] for k in range(PEER_TOPK)]
    for i, j in _sort16_pairs():
        v[i], v[j] = jnp.maximum(v[i], v[j]), jnp.minimum(v[i], v[j])
    vals = []
    for q in range(PEER_TOPK):
        m = jnp.max(v[0], axis=0, keepdims=True)
        vals.append(m)
        if q + 1 < PEER_TOPK:
            hit = v[0] == m
            for k in range(PEER_TOPK - 1 - q):
                v[k] = jnp.where(hit, v[k + 1], v[k])
    return vals


def _peer_route_kernel(x_ref, g_ref, wqt_ref, k1_ref, k2_ref,
                       h2t_ref, e1_ref, cnt_ref, rank2_ref, e2_ref):
    h2 = _rmsnorm(x_ref[...], g_ref[...])
    h2_b = h2.astype(BF16)
    h2t_ref[...] = jnp.transpose(h2).astype(BF16)
    qt = _dot_nt(wqt_ref[...], h2_b)
    for h in range(PEER_HEADS):
        q1 = qt[h * PEER_QDIM:h * PEER_QDIM + PEER_HALF].astype(BF16)
        q2 = qt[h * PEER_QDIM + PEER_HALF:(h + 1) * PEER_QDIM].astype(BF16)
        s1 = _dot(k1_ref[h], q1)
        s2 = _dot(k2_ref[h], q2)
        a1 = _top16(s1)
        a2 = _top16(s2)
        a1_mat = jnp.concatenate(a1, axis=0)
        a2_lo = jnp.concatenate(a2[:8], axis=0)
        cand = [a1[0] + a2_lo, a1[0] + jnp.concatenate(a2[8:], axis=0)]
        cand += [a1[i] + a2_lo for i in range(1, 8)]
        cand.append(a1_mat[8:] + a2[0])
        cand = jnp.concatenate(cand, axis=0)
        top = a1[0] + a2[0]
        zsum = jnp.zeros_like(top)
        thr = top
        for _ in range(PEER_TOPK):
            thr = jnp.max(cand, axis=0, keepdims=True)
            zsum = zsum + jnp.exp(thr - top)
            cand = jnp.where(cand == thr, -jnp.inf, cand)
        cnt = jnp.zeros_like(s1)
        for q in range(PEER_TOPK):
            reach = (a1_mat + a2[q]) >= thr
            b_q = jnp.min(jnp.where(reach, a1_mat, jnp.inf), axis=0, keepdims=True)
            cnt = jnp.where(s1 >= b_q, float(q + 1), cnt)
        rank2 = jnp.full(s2.shape, float(PEER_TOPK), F32)
        for q in reversed(range(PEER_TOPK)):
            rank2 = jnp.where(s2 >= a2[q], float(q), rank2)
        e1_ref[h] = jnp.exp(s1 - a1[0])
        cnt_ref[h] = cnt
        rank2_ref[h] = rank2.astype(BF16)
        e2_ref[h] = (jnp.exp(s2 - a2[0]) / zsum).astype(BF16)


def _peer_route(x1, ln2_g, wqt_bf, k1_bf, k2_bf, tt):
    n = x1.shape[0]
    keys = pl.BlockSpec((PEER_HEADS, N_KEYS, PEER_HALF), lambda i: (0, 0, 0))
    tok = pl.BlockSpec((PEER_HEADS, N_KEYS, tt), lambda i: (0, 0, i))
    per_key = lambda dt: jax.ShapeDtypeStruct((PEER_HEADS, N_KEYS, n), dt)
    return pl.pallas_call(
        _peer_route_kernel,
        grid=(n // tt,),
        in_specs=[
            pl.BlockSpec((tt, D_MODEL), lambda i: (i, 0)),
            pl.BlockSpec((1, D_MODEL), lambda i: (0, 0)),
            pl.BlockSpec((PEER_HEADS * PEER_QDIM, D_MODEL), lambda i: (0, 0)),
            keys, keys,
        ],
        out_specs=[pl.BlockSpec((D_MODEL, tt), lambda i: (0, i)), tok, tok, tok, tok],
        out_shape=[jax.ShapeDtypeStruct((D_MODEL, n), BF16), per_key(F32), per_key(F32), per_key(BF16),
                   per_key(BF16)],
        compiler_params=_params(("parallel",)),
        name="peer_route",
    )(x1, ln2_g, wqt_bf, k1_bf, k2_bf)


PEER_CHUNK = 2048
PEER_CHUNK_KEYS = PEER_CHUNK // N_KEYS
PEER_PIECE_ROWS = 16
PEER_PIECE_LANES = 128


def _peer_dense_kernel(h2t_ref, u_ref, vt_ref, e1_ref, cnt_ref, rank2_in_ref, e2_in_ref, x1_ref, g_ref,
                       y_ref, acc_scr, a_scr, w_scr, rank2_ref, e2_ref):
    c = pl.program_id(1)
    tt = h2t_ref.shape[1]

    @pl.when(c == 0)
    def _():
        acc_scr[...] = jnp.zeros_like(acc_scr)
        rank2_ref[...] = rank2_in_ref[...]
        e2_ref[...] = e2_in_ref[...]

    a_scr[...] = _dot(u_ref[...], h2t_ref[...])

    def build(key, carry):
        base = pl.multiple_of(key * N_KEYS, N_KEYS)
        cnt_rows = [cnt_ref[h, pl.ds(key, 1), :] for h in range(PEER_HEADS)]
        e1_rows = [e1_ref[h, pl.ds(key, 1), :] for h in range(PEER_HEADS)]
        for lane0 in range(0, tt, PEER_PIECE_LANES):
            lanes = slice(lane0, lane0 + PEER_PIECE_LANES)
            tile = (PEER_PIECE_ROWS, PEER_PIECE_LANES)
            cnt_b = [jnp.broadcast_to(cnt_rows[h][:, lanes], tile).astype(BF16) for h in range(PEER_HEADS)]
            e1_b = [jnp.broadcast_to(e1_rows[h][:, lanes], tile).astype(BF16) for h in range(PEER_HEADS)]
            for sub in range(0, N_KEYS, PEER_PIECE_ROWS):
                second = slice(sub, sub + PEER_PIECE_ROWS)
                gate = None
                for h in range(PEER_HEADS):
                    sel = jnp.where(rank2_ref[h, second, lanes] < cnt_b[h], e2_ref[h, second, lanes],
                                    jnp.zeros(tile, BF16))
                    term = sel * e1_b[h]
                    gate = term if gate is None else gate + term
                rows = pl.ds(base + sub, PEER_PIECE_ROWS)
                w_scr[rows, lanes] = _gelu(a_scr[rows, lanes]).astype(BF16) * gate
        return carry

    lax.fori_loop(0, PEER_CHUNK_KEYS, build, 0)
    acc_scr[...] += _dot(vt_ref[...], w_scr[...])

    @pl.when(c == pl.num_programs(1) - 1)
    def _():
        x2 = x1_ref[...] + jnp.transpose(acc_scr[...])
        y_ref[...] = _rmsnorm(x2, g_ref[...])


def _peer_dense(h2t, u_bf, vt_bf, e1, cnt, rank2, e2, x1, lnf_g, tt):
    n = x1.shape[0]
    nchunks = N_EXPERTS // PEER_CHUNK
    by_chunk = lambda z: z.reshape(PEER_HEADS, nchunks, PEER_CHUNK_KEYS, n)
    tok_all = pl.BlockSpec((PEER_HEADS, N_KEYS, tt), lambda i, c: (0, 0, i))
    tok_chunk = pl.BlockSpec((PEER_HEADS, None, PEER_CHUNK_KEYS, tt), lambda i, c: (0, c, 0, i))
    return pl.pallas_call(
        _peer_dense_kernel,
        grid=(n // tt, nchunks),
        in_specs=[
            pl.BlockSpec((D_MODEL, tt), lambda i, c: (0, i)),
            pl.BlockSpec((PEER_CHUNK, D_MODEL), lambda i, c: (c, 0)),
            pl.BlockSpec((None, D_MODEL, PEER_CHUNK), lambda i, c: (c, 0, 0)),
            tok_chunk, tok_chunk, tok_all, tok_all,
            pl.BlockSpec((tt, D_MODEL), lambda i, c: (i, 0)),
            pl.BlockSpec((1, D_MODEL), lambda i, c: (0, 0)),
        ],
        out_specs=pl.BlockSpec((tt, D_MODEL), lambda i, c: (i, 0)),
        out_shape=jax.ShapeDtypeStruct((n, D_MODEL), F32),
        scratch_shapes=[pltpu.VMEM((D_MODEL, tt), F32), pltpu.VMEM((PEER_CHUNK, tt), F32),
                        pltpu.VMEM((PEER_CHUNK, tt), BF16),
                        pltpu.VMEM((PEER_HEADS, N_KEYS, tt), BF16), pltpu.VMEM((PEER_HEADS, N_KEYS, tt), BF16)],
        compiler_params=_params(("parallel", "arbitrary")),
        name="peer_dense",
    )(h2t, u_bf, vt_bf, by_chunk(e1), by_chunk(cnt), rank2, e2, x1, lnf_g)


def _pick_tile(n, target):
    t = min(n, target)
    while n % t:
        t //= 2
    return t


def _layer(x, wkv0, shift0, s5re0, s5im0, prm, s5m, lnf_g):
    bsz, t_len, _ = x.shape
    n = bsz * t_len
    x2d = x.reshape(n, D_MODEL)
    tm = _pick_tile(t_len, 512)
    prw, u = _in_proj(x2d, prm["ln1_g"], prm["w_in"], tm)

    tiles_per_seq = t_len // tm
    prw3 = prw.reshape(bsz, t_len, RWKV_PROJ)
    last_rows = prw3[:, tm - 1::tm, :]
    bound = jnp.concatenate([shift0, last_rows[:, :tiles_per_seq - 1, :]], axis=1)
    bound = bound.reshape(bsz * tiles_per_seq, 1, RWKV_PROJ)
    r, lw, k, v, kk, kka, g, bonus = _rwkv_prep(prw, bound, prm, tm)

    t_pad = -(-t_len // WKV_CHUNK) * WKV_CHUNK
    seq3 = lambda z: jnp.pad(z.reshape(bsz, t_len, RWKV_WIDTH), ((0, 0), (0, t_pad - t_len), (0, 0)))
    st = wkv0.reshape(bsz, HEAD_PAIRS, 2, HEAD_DIM, HEAD_DIM).transpose(0, 1, 2, 4, 3)
    eye2 = jnp.eye(2, dtype=F32)
    s0_bd = jnp.einsum('bphkv,hg->bphkgv', st, eye2).reshape(bsz, HEAD_PAIRS, PAIR_LANES, PAIR_LANES)
    o, s1_bd = _wkv(seq3(r), seq3(lw), seq3(k), seq3(v), seq3(kk), seq3(kka), s0_bd)
    o = o[:, :t_len].reshape(n, RWKV_WIDTH)
    s1 = s1_bd.reshape(bsz, HEAD_PAIRS, 2, HEAD_DIM, 2, HEAD_DIM)
    s1 = jnp.stack([s1[:, :, 0, :, 0, :], s1[:, :, 1, :, 1, :]], axis=2)
    wkv1 = s1.transpose(0, 1, 2, 4, 3).reshape(bsz, RWKV_HEADS, HEAD_DIM, HEAD_DIM)

    nc = t_len // S5_CHUNK
    blk = lambda z: z.reshape(bsz, S5_LANE_BLOCKS, S5_BLOCK_STATE).transpose(1, 0, 2)
    x0 = jnp.concatenate([blk(s5re0), blk(s5im0)], axis=2)
    y3, x1s = _s5(u.reshape(bsz * nc, S5_CHUNK, S5_WIDTH), x0, s5m, nc)
    y = y3.reshape(n, S5_WIDTH)
    unblk = lambda z: z.transpose(1, 0, 2).reshape(bsz, S5_GROUPS, S5_STATE)
    s5re1, s5im1 = unblk(x1s[:, :, :S5_BLOCK_STATE]), unblk(x1s[:, :, S5_BLOCK_STATE:])

    x1 = _mix_out(o, bonus, g, y, x2d, prm, tm)

    tt = _pick_tile(n, 512)
    h2t, e1, cnt, rank2, e2 = _peer_route(x1, prm["ln2_g"], prm["w_qt"], prm["keys1"], prm["keys2"], tt)
    yout = _peer_dense(h2t, prm["peer_u"], prm["peer_vt"], e1, cnt, rank2, e2, x1, lnf_g, tt)
    shift1 = prw3[:, -1:, :]
    return yout.reshape(bsz, t_len, D_MODEL), wkv1, shift1, s5re1, s5im1


def kernel(x_prompt, x_sample, state_wkv, state_shift, state_s5_re, state_s5_im, ln1_g, w_in, rwkv_mu, rwkv_w0, rwkv_w2, rwkv_a0, rwkv_a2, rwkv_g2, rwkv_k_k, rwkv_k_a, rwkv_r_k, rwkv_gn_w, rwkv_gn_b, s5_a_re, s5_a_im, s5_log_dt, s5_b_re, s5_b_im, s5_c_re, s5_c_im, s5_d, s5_w_glu, s5_b_glu, w_out, ln2_g, peer_w_q, peer_keys1, peer_keys2, peer_u, peer_v, lnf_g):
    depth = w_in.shape[0]
    assert depth == 1
    l = 0
    row = lambda z: z.reshape(1, -1).astype(F32)
    lora = jnp.zeros((LORA_WIDTH, 3 * RWKV_WIDTH), F32)
    lora = lora.at[:64, :RWKV_WIDTH].set(rwkv_w2[l])
    lora = lora.at[64:128, RWKV_WIDTH:2 * RWKV_WIDTH].set(rwkv_a2[l])
    lora = lora.at[128:, 2 * RWKV_WIDTH:].set(rwkv_g2[l])
    lane = jnp.arange(RWKV_WIDTH)
    ones_bd = (lane[:, None] // HEAD_DIM == lane[None, :] // HEAD_DIM).astype(BF16)
    prm = {
        "ln1_g": row(ln1_g[l]), "w_in": w_in[l].astype(BF16), "mu": row(rwkv_mu[l]),
        "w0": row(rwkv_w0[l]), "a0": row(rwkv_a0[l]), "lora": lora.astype(BF16),
        "k_k": row(rwkv_k_k[l]), "k_a": row(rwkv_k_a[l]), "r_k": row(rwkv_r_k[l]),
        "gn_w": row(rwkv_gn_w[l]), "gn_b": row(rwkv_gn_b[l]), "ones_bd": ones_bd,
        "w_glu": s5_w_glu[l].astype(BF16), "b_glu": row(s5_b_glu[l]), "w_out": w_out[l].astype(BF16),
        "ln2_g": row(ln2_g[l]), "w_qt": peer_w_q[l].T.astype(BF16),
        "keys1": peer_keys1[l].astype(BF16), "keys2": peer_keys2[l].astype(BF16),
        "peer_u": peer_u[l].astype(BF16),
        "peer_vt": peer_v[l].astype(BF16).reshape(N_EXPERTS // PEER_CHUNK, PEER_CHUNK, D_MODEL)
        .transpose(0, 2, 1),
    }
    s5m = _s5_matrices(s5_a_re[l], s5_a_im[l], s5_log_dt[l], s5_b_re[l], s5_b_im[l], s5_c_re[l],
                       s5_c_im[l], s5_d[l])
    lnf = row(lnf_g)
    bp = x_prompt.shape[0]
    zeros = lambda *s: jnp.zeros(s, F32)
    yp, a1, a2, a3, a4 = _layer(
        x_prompt, zeros(bp, RWKV_HEADS, HEAD_DIM, HEAD_DIM), zeros(bp, 1, RWKV_PROJ),
        zeros(bp, S5_GROUPS, S5_STATE), zeros(bp, S5_GROUPS, S5_STATE), prm, s5m, lnf)
    ys, b1, b2, b3, b4 = _layer(x_sample, state_wkv[l], state_shift[l], state_s5_re[l], state_s5_im[l],
                                prm, s5m, lnf)
    st = lambda z: z[None]
    return (yp, ys, st(a1), st(a2), st(a3), st(a4), st(b1), st(b2), st(b3), st(b4))
```
